```python
import math
import jax
import jax.numpy as jnp
from jax import lax
import numpy as np

D_MODEL = 4096
BATCH = 4
SEQ = 2048
DEPTH = 2
DEC_BATCH = 128
DEC_SEQ = 1
PAST_LEN = 16384
PAGE_SIZE = 128

N_EVEN = (DEPTH + 1) // 2
N_ODD = DEPTH // 2

HEAD_DIM = 128
ROPE_THETA = 500000.0
Q_BLOCK = 128
EPS = 1e-6

H_A = D_MODEL // 2 // HEAD_DIM
HD_A = HEAD_DIM
KVH_A = 1
ROT_A = HD_A // 4
H_IDX = 32
D_IDX = 64
ROT_IDX = D_IDX // 4
TOPK_MAX = 256

H_B = D_MODEL // 2 // HEAD_DIM
D_NOPE = 128
D_ROPE_B = 64
D_VB = 128
Q_RANK = 768
KV_RANK = 256

H_C = D_MODEL // HEAD_DIM
HD_C = HEAD_DIM
KVH_C = 2

FFN_DIM = 256 * ((8 * D_MODEL // 3 + 255) // 256)

EVEN_SPLITS = (H_A * HD_A, KVH_A * HD_A, KVH_A * HD_A, H_IDX * D_IDX, H_IDX, D_IDX, Q_RANK, KV_RANK + D_ROPE_B)
ODD_IN = H_C * HD_C + 2 * KVH_C * HD_C

kernel_name = 'hybrid_dsa_mla_stickbreak_macaron_step'


def rms_norm(x, g):
    x32 = x.astype(jnp.float32)
    y = x32 * lax.rsqrt(jnp.mean(x32 * x32, axis=-1, keepdims=True) + EPS)
    return (y * g.astype(jnp.float32)).astype(x.dtype)


def rope(x, pos, rot):
    half = rot // 2
    inv = ROPE_THETA ** (-jnp.arange(half, dtype=jnp.float32) / half)
    ang = pos[:, None] * inv[None, :]
    cos = jnp.cos(ang)[:, None, :].astype(x.dtype)
    sin = jnp.sin(ang)[:, None, :].astype(x.dtype)
    x1, x2 = x[..., :half], x[..., half:rot]
    return jnp.concatenate([x1 * cos - x2 * sin, x2 * cos + x1 * sin, x[..., rot:]], axis=-1)


def swiglu(h, w_in, w_out):
    g, u = jnp.split(h @ w_in, 2, axis=-1)
    return (jax.nn.silu(g) * u) @ w_out


def masked_softmax(logits, mask):
    return jax.nn.softmax(jnp.where(mask, logits.astype(jnp.float32), -jnp.inf), axis=-1)


def gather_pages(pool, layer, page_table):
    rows = pool[layer, page_table]
    return rows.reshape((page_table.shape[0], page_table.shape[1] * pool.shape[2]) + pool.shape[3:])


def over_query_blocks(fn, n_q, q_offset, *arrays):
    qb = math.gcd(n_q, Q_BLOCK)
    nb = n_q // qb
    xs = tuple(a.reshape((a.shape[0], nb, qb) + a.shape[2:]).swapaxes(0, 1) for a in arrays)
    starts = q_offset + qb * jnp.arange(nb, dtype=jnp.int32)
    out = lax.map(lambda args: fn(*args), (starts,) + xs)
    out = out.swapaxes(0, 1)
    return out.reshape((out.shape[0], n_q) + out.shape[3:])


def dsa_mla_mixer(h, pos0, past, w_in, a_qnorm, a_knorm, idx_knorm, b_qa_norm, b_wq_b,
                  b_kv_norm, b_wkv_b, b_qnorm, b_krnorm, w_out):
    B, T, _ = h.shape
    pos = jnp.arange(T, dtype=jnp.float32) + pos0
    offs = np.cumsum(EVEN_SPLITS)[:-1].tolist()
    qa, ka, va, iq, iw, ik, qlo, kvlo = jnp.split(h @ w_in, offs, axis=-1)

    qa = rope(rms_norm(qa.reshape(B, T, H_A, HD_A), a_qnorm), pos, ROT_A)
    ka = rope(rms_norm(ka.reshape(B, T, KVH_A, HD_A), a_knorm), pos, ROT_A)
    a_kv = jnp.stack([ka, va.reshape(B, T, KVH_A, HD_A)], axis=2)
    iq = rope(iq.reshape(B, T, H_IDX, D_IDX), pos, ROT_IDX)
    iw = iw * (H_IDX * D_IDX) ** -0.5
    ik = rope(rms_norm(ik, idx_knorm)[:, :, None, :], pos, ROT_IDX)[:, :, 0, :]

    qb = (rms_norm(qlo, b_qa_norm) @ b_wq_b).reshape(B, T, H_B, D_NOPE + D_ROPE_B)
    qb = rms_norm(qb, b_qnorm)
    q_nope, q_rope = qb[..., :D_NOPE], rope(qb[..., D_NOPE:], pos, D_ROPE_B)
    c = rms_norm(kvlo[..., :KV_RANK], b_kv_norm)
    kr = rope(rms_norm(kvlo[..., KV_RANK:], b_krnorm)[:, :, None, :], pos, D_ROPE_B)[:, :, 0, :]
    b_lat = jnp.concatenate([c, kr], axis=-1)
    w_uk, w_uv = b_wkv_b[..., :D_NOPE], b_wkv_b[..., D_NOPE:]
    q_lat = jnp.concatenate([jnp.einsum('bthd,rhd->bthr', q_nope, w_uk), q_rope], axis=-1)

    if past is None:
        idx_segs, lat_segs = (ik,), (b_lat,)

        def fetch(sel):
            return jax.vmap(lambda kv, s: kv[s])(a_kv, sel)
    else:
        page_table, layer, pool_kv, pool_idx, pool_lat = past
        idx_segs = (gather_pages(pool_idx, layer, page_table), ik)
        lat_segs = (gather_pages(pool_lat, layer, page_table), b_lat)

        def fetch(sel):
            ps = jnp.minimum(sel, pos0 - 1)
            page = jnp.take_along_axis(page_table, (ps // PAGE_SIZE).reshape(B, -1), axis=1).reshape(sel.shape)
            old = pool_kv[layer, page, ps % PAGE_SIZE]
            new = jax.vmap(lambda kv, s: kv[s])(a_kv, jnp.clip(sel - pos0, 0, T - 1))
            return jnp.where((sel < pos0)[..., None, None, None], old, new)

    topk = min(TOPK_MAX, (pos0 + T) // 4)

    def dsa_block(q0, q, iqb, iwb):
        qpos = q0 + jnp.arange(q.shape[1])
        sc = jnp.concatenate([jnp.einsum('bqhd,bkd->bqhk', iqb, s) for s in idx_segs], axis=-1)
        score = jnp.einsum('bqh,bqhk->bqk', iwb.astype(jnp.float32), jax.nn.relu(sc).astype(jnp.float32))
        kpos = jnp.arange(score.shape[-1])
        score = jnp.where(kpos[None, None, :] <= qpos[None, :, None], score, -jnp.inf)
        _, sel = lax.top_k(score, topk)
        kv = fetch(sel)
        qg = q.reshape(q.shape[:2] + (KVH_A, H_A // KVH_A, HD_A))
        logits = jnp.einsum('bqngd,bqsnd->bqngs', qg, kv[:, :, :, 0]) * HD_A ** -0.5
        valid = (sel <= qpos[None, :, None])[:, :, None, None, :]
        p = masked_softmax(logits, valid).astype(q.dtype)
        o = jnp.einsum('bqngs,bqsnd->bqngd', p, kv[:, :, :, 1])
        return o.reshape(q.shape)

    def mla_block(q0, ql):
        qpos = q0 + jnp.arange(ql.shape[1])
        logits = jnp.concatenate([jnp.einsum('bqhr,bkr->bqhk', ql, s) for s in lat_segs], axis=-1)
        logits = logits * (D_NOPE + D_ROPE_B) ** -0.5
        kpos = jnp.arange(logits.shape[-1])
        p = masked_softmax(logits, (kpos[None, :] <= qpos[:, None])[None, :, None, :]).astype(ql.dtype)
        ctx, start = 0, 0
        for s in lat_segs:
            n = s.shape[1]
            ctx = ctx + jnp.einsum('bqhk,bkr->bqhr', p[..., start:start + n], s[..., :KV_RANK])
            start += n
        return ctx

    oa = over_query_blocks(dsa_block, T, pos0, qa, iq, iw)
    ctx = over_query_blocks(mla_block, T, pos0, q_lat)
    ob = jnp.einsum('bthr,rhd->bthd', ctx, w_uv)
    mix = jnp.concatenate([oa.reshape(B, T, -1), ob.reshape(B, T, -1)], axis=-1) @ w_out
    return mix, a_kv, ik, b_lat


def stick_breaking_mixer(h, pos0, past, w_in, w_out):
    B, T, _ = h.shape
    q, k, v = jnp.split(h @ w_in, [H_C * HD_C, H_C * HD_C + KVH_C * HD_C], axis=-1)
    q = q.reshape(B, T, H_C, HD_C)
    c_kv = jnp.stack([k.reshape(B, T, KVH_C, HD_C), v.reshape(B, T, KVH_C, HD_C)], axis=2)
    if past is None:
        kv_segs = (c_kv,)
    else:
        page_table, layer, pool_kv = past
        kv_segs = (gather_pages(pool_kv, layer, page_table), c_kv)

    def sb_block(q0, qb):
        qpos = q0 + jnp.arange(qb.shape[1])
        qg = qb.reshape(qb.shape[:2] + (KVH_C, H_C // KVH_C, HD_C))
        z = jnp.concatenate([jnp.einsum('bqngd,bsnd->bqngs', qg, s[:, :, 0]) for s in kv_segs], axis=-1)
        z = z.astype(jnp.float32) * HD_C ** -0.5
        kpos = jnp.arange(z.shape[-1])
        before = (kpos[None, :] < qpos[:, None])[None, :, None, None, :]
        log_keep = jnp.where(before, jax.nn.log_sigmoid(-z), 0.0)
        log_w = jax.nn.log_sigmoid(z) + lax.cumsum(log_keep, axis=z.ndim - 1, reverse=True) - log_keep
        a = jnp.where(before, jnp.exp(log_w), 0.0).astype(qb.dtype)
        o, start = 0, 0
        for s in kv_segs:
            n = s.shape[1]
            o = o + jnp.einsum('bqngs,bsnd->bqngd', a[..., start:start + n], s[:, :, 1])
            start += n
        return o.reshape(qb.shape)

    o = over_query_blocks(sb_block, T, pos0, q)
    return o.reshape(B, T, -1) @ w_out, c_kv


def setup_inputs(seed: int = 0) -> dict:
    key = jax.random.key(seed)
    ks = list(jax.random.split(key, 32))
    f32 = jnp.float32
    n_pages = PAST_LEN // PAGE_SIZE
    n_used = DEC_BATCH * n_pages
    n_pool = n_used + max(1, n_used // 4)
    even_in = sum(EVEN_SPLITS)

    def nrm(shape, scale=1.0):
        return jax.random.normal(ks.pop(), shape, f32) * scale

    def gain(shape):
        return 1.0 + 0.02 * jax.random.normal(ks.pop(), shape, f32)

    inp = {}
    inp['x_prompt'] = nrm((BATCH, SEQ, D_MODEL))
    inp['x_sample'] = nrm((DEC_BATCH, DEC_SEQ, D_MODEL))
    inp['cache_a_kv'] = nrm((N_EVEN, n_pool, PAGE_SIZE, 2, KVH_A, HD_A))
    inp['cache_a_idx'] = nrm((N_EVEN, n_pool, PAGE_SIZE, D_IDX))
    inp['cache_b_latent'] = nrm((N_EVEN, n_pool, PAGE_SIZE, KV_RANK + D_ROPE_B))
    inp['cache_c_kv'] = nrm((N_ODD, n_pool, PAGE_SIZE, 2, KVH_C, HD_C))
    inp['page_table'] = jax.random.permutation(ks.pop(), n_pool)[:n_used].reshape(DEC_BATCH, n_pages).astype(jnp.int32)
    inp['ffn1_norm'] = gain((DEPTH, D_MODEL))
    inp['ffn1_w_in'] = nrm((DEPTH, D_MODEL, 2 * FFN_DIM), D_MODEL ** -0.5)
    inp['ffn1_w_out'] = nrm((DEPTH, FFN_DIM, D_MODEL), FFN_DIM ** -0.5)
    inp['ffn2_norm'] = gain((DEPTH, D_MODEL))
    inp['ffn2_w_in'] = nrm((DEPTH, D_MODEL, 2 * FFN_DIM), D_MODEL ** -0.5)
    inp['ffn2_w_out'] = nrm((DEPTH, FFN_DIM, D_MODEL), FFN_DIM ** -0.5)
    inp['e_norm'] = gain((N_EVEN, D_MODEL))
    inp['e_w_in'] = nrm((N_EVEN, D_MODEL, even_in), D_MODEL ** -0.5)
    inp['e_a_qnorm'] = gain((N_EVEN, HD_A))
    inp['e_a_knorm'] = gain((N_EVEN, HD_A))
    inp['e_idx_knorm'] = gain((N_EVEN, D_IDX))
    inp['e_b_qa_norm'] = gain((N_EVEN, Q_RANK))
    inp['e_b_wq_b'] = nrm((N_EVEN, Q_RANK, H_B * (D_NOPE + D_ROPE_B)), Q_RANK ** -0.5)
    inp['e_b_kv_norm'] = gain((N_EVEN, KV_RANK))
    inp['e_b_wkv_b'] = nrm((N_EVEN, KV_RANK, H_B, D_NOPE + D_VB), KV_RANK ** -0.5)
    inp['e_b_qnorm'] = gain((N_EVEN, D_NOPE + D_ROPE_B))
    inp['e_b_krnorm'] = gain((N_EVEN, D_ROPE_B))
    inp['e_w_out'] = nrm((N_EVEN, H_A * HD_A + H_B * D_VB, D_MODEL), (H_A * HD_A + H_B * D_VB) ** -0.5)
    inp['o_norm'] = gain((N_ODD, D_MODEL))
    inp['o_w_in'] = nrm((N_ODD, D_MODEL, ODD_IN), D_MODEL ** -0.5)
    inp['o_w_out'] = nrm((N_ODD, H_C * HD_C, D_MODEL), (H_C * HD_C) ** -0.5)
    return inp


def reference(x_prompt, x_sample, cache_a_kv, cache_a_idx, cache_b_latent, cache_c_kv, page_table,
              ffn1_norm, ffn1_w_in, ffn1_w_out, ffn2_norm, ffn2_w_in, ffn2_w_out,
              e_norm, e_w_in, e_a_qnorm, e_a_knorm, e_idx_knorm, e_b_qa_norm, e_b_wq_b,
              e_b_kv_norm, e_b_wkv_b, e_b_qnorm, e_b_krnorm, e_w_out,
              o_norm, o_w_in, o_w_out):
    past_len = page_table.shape[1] * PAGE_SIZE
    xp, xs = x_prompt, x_sample
    akv_p, akv_s, aidx_p, aidx_s, blat_p, blat_s, ckv_p, ckv_s = [], [], [], [], [], [], [], []
    for layer in range(DEPTH):
        xp = xp + 0.5 * swiglu(rms_norm(xp, ffn1_norm[layer]), ffn1_w_in[layer], ffn1_w_out[layer])
        xs = xs + 0.5 * swiglu(rms_norm(xs, ffn1_norm[layer]), ffn1_w_in[layer], ffn1_w_out[layer])
        if layer % 2 == 0:
            e = layer // 2
            ew = (e_w_in[e], e_a_qnorm[e], e_a_knorm[e], e_idx_knorm[e], e_b_qa_norm[e], e_b_wq_b[e],
                  e_b_kv_norm[e], e_b_wkv_b[e], e_b_qnorm[e], e_b_krnorm[e], e_w_out[e])
            mp, kv_p, ix_p, lt_p = dsa_mla_mixer(rms_norm(xp, e_norm[e]), 0, None, *ew)
            ms, kv_s, ix_s, lt_s = dsa_mla_mixer(
                rms_norm(xs, e_norm[e]), past_len,
                (page_table, e, cache_a_kv, cache_a_idx, cache_b_latent), *ew)
            akv_p.append(kv_p)
            akv_s.append(kv_s)
            aidx_p.append(ix_p)
            aidx_s.append(ix_s)
            blat_p.append(lt_p)
            blat_s.append(lt_s)
        else:
            o = layer // 2
            mp, c_p = stick_breaking_mixer(rms_norm(xp, o_norm[o]), 0, None, o_w_in[o], o_w_out[o])
            ms, c_s = stick_breaking_mixer(rms_norm(xs, o_norm[o]), past_len,
                                           (page_table, o, cache_c_kv), o_w_in[o], o_w_out[o])
            ckv_p.append(c_p)
            ckv_s.append(c_s)
        xp = xp + mp
        xs = xs + ms
        xp = xp + 0.5 * swiglu(rms_norm(xp, ffn2_norm[layer]), ffn2_w_in[layer], ffn2_w_out[layer])
        xs = xs + 0.5 * swiglu(rms_norm(xs, ffn2_norm[layer]), ffn2_w_in[layer], ffn2_w_out[layer])
    return (xp, xs, jnp.stack(akv_p), jnp.stack(akv_s), jnp.stack(aidx_p), jnp.stack(aidx_s),
            jnp.stack(blat_p), jnp.stack(blat_s), jnp.stack(ckv_p), jnp.stack(ckv_s))
```

```python
import functools

import numpy as np
import jax
import jax.numpy as jnp
from jax import lax
from jax.experimental import pallas as pl
from jax.experimental.pallas import tpu as pltpu

HEAD_DIM = 128
ROPE_THETA = 500000.0
EPS = 1e-6
PAGE_SIZE = 128
H_IDX = 32
D_IDX = 64
TOPK_MAX = 256
Q_RANK = 768
KV_RANK = 256
D_NOPE = 128
D_ROPE_B = 64
D_VB = 128
KVH_C = 2
LANES = 128
V7X_VMEM_LIMIT = 56 * 1024 * 1024
NEG = -1e30
INT_MIN = -(2 ** 31)

BF16 = jnp.bfloat16
F32 = jnp.float32


def _pick(n, target, mult):
    best = None
    for d in range(mult, min(n, target) + 1, mult):
        if n % d == 0:
            best = d
    assert best is not None, (n, target, mult)
    return best


def _params(*sem):
    return pltpu.CompilerParams(dimension_semantics=sem, vmem_limit_bytes=V7X_VMEM_LIMIT)


def _dot(a, b):
    return jnp.dot(a, b, preferred_element_type=F32)


def _dot_nt(a, b):
    return lax.dot_general(a, b, (((1,), (1,)), ((), ())), preferred_element_type=F32)


def _softplus(z):
    return jnp.maximum(z, 0.0) + jnp.log1p(jnp.exp(-jnp.abs(z)))


def _norm_kernel(x_ref, g_ref, o_ref):
    x = x_ref[...]
    y = x * lax.rsqrt(jnp.mean(x * x, axis=-1, keepdims=True) + EPS)
    o_ref[...] = (y * g_ref[...]).astype(o_ref.dtype)


def rms_norm_rows(x, g):
    m, d = x.shape
    tm = _pick(m, 512, 16)
    return pl.pallas_call(
        _norm_kernel,
        grid=(m // tm,),
        in_specs=[pl.BlockSpec((tm, d), lambda i: (i, 0)), pl.BlockSpec((1, d), lambda i: (0, 0))],
        out_specs=pl.BlockSpec((tm, d), lambda i: (i, 0)),
        out_shape=jax.ShapeDtypeStruct((m, d), BF16),
        compiler_params=_params("parallel"),
        name="rms_norm_rows",
    )(x, g.reshape(1, d))


def _mm_kernel(a_ref, w_ref, o_ref):
    o_ref[...] = _dot(a_ref[...], w_ref[...]).astype(o_ref.dtype)


def matmul(a, w, out_dtype=F32, tm_target=1664, tn_target=512):
    m, k = a.shape
    n = w.shape[1]
    tm = _pick(m, tm_target, 16)
    tn = _pick(n, tn_target, LANES)
    return pl.pallas_call(
        _mm_kernel,
        grid=(m // tm, n // tn),
        in_specs=[pl.BlockSpec((tm, k), lambda i, j: (i, 0)), pl.BlockSpec((k, tn), lambda i, j: (0, j))],
        out_specs=pl.BlockSpec((tm, tn), lambda i, j: (i, j)),
        out_shape=jax.ShapeDtypeStruct((m, n), out_dtype),
        compiler_params=_params("parallel", "parallel"),
        name="matmul",
    )(a, w)


def _mm_res_kernel(a_ref, w_ref, r_ref, o_ref, *, scale):
    o_ref[...] = r_ref[...] + scale * _dot(a_ref[...], w_ref[...])


def matmul_residual(a, w, res, scale, tm_target=416, tn_target=512):
    m, k = a.shape
    n = w.shape[1]
    tm = _pick(m, tm_target, 16)
    tn = _pick(n, tn_target, LANES)
    return pl.pallas_call(
        functools.partial(_mm_res_kernel, scale=scale),
        grid=(m // tm, n // tn),
        in_specs=[pl.BlockSpec((tm, k), lambda i, j: (i, 0)), pl.BlockSpec((k, tn), lambda i, j: (0, j)),
                  pl.BlockSpec((tm, tn), lambda i, j: (i, j))],
        out_specs=pl.BlockSpec((tm, tn), lambda i, j: (i, j)),
        out_shape=jax.ShapeDtypeStruct((m, n), F32),
        compiler_params=_params("parallel", "parallel"),
        name="matmul_residual",
    )(a, w, res)


def _ffn_in_kernel(a_ref, wg_ref, wu_ref, o_ref):
    a = a_ref[...]
    g = _dot(a, wg_ref[...])
    u = _dot(a, wu_ref[...])
    o_ref[...] = (g * jax.nn.sigmoid(g) * u).astype(o_ref.dtype)


def ffn_in(h, w_in, tm_target=1664, tn_target=256):
    m, k = h.shape
    f = w_in.shape[1] // 2
    tm = _pick(m, tm_target, 16)
    tn = _pick(f, tn_target, LANES)
    nj = f // tn
    return pl.pallas_call(
        _ffn_in_kernel,
        grid=(m // tm, nj),
        in_specs=[pl.BlockSpec((tm, k), lambda i, j: (i, 0)), pl.BlockSpec((k, tn), lambda i, j: (0, j)),
                  pl.BlockSpec((k, tn), lambda i, j: (0, j + nj))],
        out_specs=pl.BlockSpec((tm, tn), lambda i, j: (i, j)),
        out_shape=jax.ShapeDtypeStruct((m, f), BF16),
        compiler_params=_params("parallel", "parallel"),
        name="ffn_in",
    )(h, w_in, w_in)


def ffn_half_step(x, g, w_in, w_out):
    h = rms_norm_rows(x, g)
    act = ffn_in(h, w_in)
    return matmul_residual(act, w_out, x, 0.5)


def _rope_rows(rot, width):
    half = rot // 2
    j = np.arange(LANES) % width
    in_rot = (j < rot).astype(np.float32)
    take_lo = ((j >= half) & (j < rot)).astype(np.float32)
    take_hi = (j < half).astype(np.float32)
    inv = ROPE_THETA ** (-jnp.arange(half, dtype=F32) / half)
    inv_row = jnp.where(jnp.asarray(j < rot), inv[j % half], 0.0).astype(F32)
    return jnp.stack([inv_row, jnp.asarray(in_rot), jnp.asarray(take_lo), jnp.asarray(take_hi)])


def _rope_table_kernel(pos_ref, rows_ref, o_ref):
    pos = pos_ref[...]
    for f in range(rows_ref.shape[0]):
        rows = rows_ref[f]
        ang = pos * rows[0:1]
        cos = jnp.cos(ang)
        sin = jnp.sin(ang)
        o_ref[f, 0] = jnp.where(rows[1:2] > 0.5, cos, 1.0)
        o_ref[f, 1] = rows[2:3] * sin
        o_ref[f, 2] = -rows[3:4] * sin


def rope_tables(pos, rows):
    m = pos.shape[0]
    nf = rows.shape[0]
    tm = _pick(m, 512, 8)
    return pl.pallas_call(
        _rope_table_kernel,
        grid=(m // tm,),
        in_specs=[pl.BlockSpec((tm, 1), lambda i: (i, 0)), pl.BlockSpec((nf, 4, LANES), lambda i: (0, 0, 0))],
        out_specs=pl.BlockSpec((nf, 3, tm, LANES), lambda i: (0, 0, i, 0)),
        out_shape=jax.ShapeDtypeStruct((nf, 3, m, LANES), F32),
        compiler_params=_params("parallel"),
        name="rope_tables",
    )(pos, rows)


def _rope(x, tab, half):
    return x * tab[0] + pltpu.roll(x, half, 1) * tab[1] + pltpu.roll(x, LANES - half, 1) * tab[2]


def _head_norm(x, g, n):
    return x * lax.rsqrt(jnp.sum(x * x, axis=-1, keepdims=True) / n + EPS) * g


def _even_post_kernel(p_ref, tab_ref, gq_ref, gk_ref, gik_ref, gqa_ref, gkv_ref, gkr_ref,
                      qa_o, kv_o, iq_o, iw_o, ik_o, ika_o, ikb_o, qlo_o, lat_o, *, h_a, offs):
    o_qa, o_ka, o_va, o_iq, o_iw, o_ik, o_qlo, o_c, o_kr = offs
    tab_a = tab_ref[0]
    tab_i = tab_ref[1]
    tab_b = tab_ref[2]
    rot_a = HEAD_DIM // 4
    rot_i = D_IDX // 4
    for h in range(h_a):
        x = p_ref[:, o_qa + LANES * h:o_qa + LANES * (h + 1)]
        y = _rope(_head_norm(x, gq_ref[...], HEAD_DIM), tab_a, rot_a // 2)
        qa_o[:, LANES * h:LANES * (h + 1)] = y.astype(qa_o.dtype)
    ka = _rope(_head_norm(p_ref[:, o_ka:o_ka + LANES], gk_ref[...], HEAD_DIM), tab_a, rot_a // 2)
    kv_o[:, 0:LANES] = ka
    kv_o[:, LANES:2 * LANES] = p_ref[:, o_va:o_va + LANES]
    for b in range(H_IDX * D_IDX // LANES):
        x = p_ref[:, o_iq + LANES * b:o_iq + LANES * (b + 1)]
        iq_o[:, LANES * b:LANES * (b + 1)] = _rope(x, tab_i, rot_i // 2).astype(iq_o.dtype)
    iw_o[...] = p_ref[:, o_iw:o_iw + LANES] * (H_IDX * D_IDX) ** -0.5
    ik = _rope(_head_norm(p_ref[:, o_ik:o_ik + LANES], gik_ref[...], D_IDX), tab_i, rot_i // 2)
    ik_o[...] = ik[:, :D_IDX]
    ika_o[...] = ik.astype(ika_o.dtype)
    ikb_o[...] = pltpu.roll(ik, D_IDX, 1).astype(ikb_o.dtype)
    qlo = p_ref[:, o_qlo:o_qlo + Q_RANK]
    qlo_o[...] = _head_norm(qlo, gqa_ref[...], Q_RANK).astype(qlo_o.dtype)
    lat_o[:, 0:KV_RANK] = _head_norm(p_ref[:, o_c:o_c + KV_RANK], gkv_ref[...], KV_RANK)
    kr = _rope(_head_norm(p_ref[:, o_kr:o_kr + LANES], gkr_ref[...], D_ROPE_B), tab_b, D_ROPE_B // 2)
    lat_o[:, KV_RANK:KV_RANK + D_ROPE_B] = kr[:, :D_ROPE_B]


def even_post(proj, tabs, gains, h_a, offs):
    m, npj = proj.shape
    tm = _pick(m, 320, 16)
    row = lambda w: pl.BlockSpec((tm, w), lambda i: (i, 0))
    full = lambda a: pl.BlockSpec(a.shape, lambda i: (0,) * a.ndim)
    out_shape = [
        jax.ShapeDtypeStruct((m, h_a * HEAD_DIM), BF16),
        jax.ShapeDtypeStruct((m, 2 * HEAD_DIM), F32),
        jax.ShapeDtypeStruct((m, H_IDX * D_IDX), BF16),
        jax.ShapeDtypeStruct((m, LANES), F32),
        jax.ShapeDtypeStruct((m, D_IDX), F32),
        jax.ShapeDtypeStruct((m, LANES), BF16),
        jax.ShapeDtypeStruct((m, LANES), BF16),
        jax.ShapeDtypeStruct((m, Q_RANK), BF16),
        jax.ShapeDtypeStruct((m, KV_RANK + D_ROPE_B), F32),
    ]
    return pl.pallas_call(
        functools.partial(_even_post_kernel, h_a=h_a, offs=offs),
        grid=(m // tm,),
        in_specs=[row(npj), pl.BlockSpec((3, 3, tm, LANES), lambda i: (0, 0, i, 0))] + [full(g) for g in gains],
        out_specs=[row(s.shape[1]) for s in out_shape],
        out_shape=out_shape,
        compiler_params=_params("parallel"),
        name="even_post",
    )(proj, tabs, *gains)


def _mla_q_kernel(qb_ref, tab_ref, gn_ref, gr_ref, wuk_ref, o_ref, *, h_b):
    tab_b = tab_ref[0]
    qw = KV_RANK + LANES
    for h in range(h_b):
        xn = qb_ref[:, LANES * h:LANES * (h + 1)]
        xr = qb_ref[:, LANES * (h_b + h):LANES * (h_b + h + 1)]
        ss = jnp.sum(xn * xn, axis=-1, keepdims=True) + jnp.sum(xr * xr, axis=-1, keepdims=True)
        r = lax.rsqrt(ss / (D_NOPE + D_ROPE_B) + EPS)
        yn = xn * r * gn_ref[...]
        yr = _rope(xr * r * gr_ref[...], tab_b, D_ROPE_B // 2)
        o_ref[:, qw * h:qw * h + KV_RANK] = _dot(yn.astype(BF16), wuk_ref[h]).astype(o_ref.dtype)
        o_ref[:, qw * h + KV_RANK:qw * (h + 1)] = yr.astype(o_ref.dtype)


def mla_q(qb, tabs, gn, gr, wuk, h_b):
    m = qb.shape[0]
    tm = _pick(m, 320, 16)
    qw = KV_RANK + LANES
    return pl.pallas_call(
        functools.partial(_mla_q_kernel, h_b=h_b),
        grid=(m // tm,),
        in_specs=[pl.BlockSpec((tm, qb.shape[1]), lambda i: (i, 0)),
                  pl.BlockSpec((1, 3, tm, LANES), lambda i: (2, 0, i, 0)),
                  pl.BlockSpec((1, LANES), lambda i: (0, 0)), pl.BlockSpec((1, LANES), lambda i: (0, 0)),
                  pl.BlockSpec(wuk.shape, lambda i: (0, 0, 0))],
        out_specs=pl.BlockSpec((tm, h_b * qw), lambda i: (i, 0)),
        out_shape=jax.ShapeDtypeStruct((m, h_b * qw), BF16),
        compiler_params=_params("parallel"),
        name="mla_q",
    )(qb, tabs, gn, gr, wuk)


def _head_out_kernel(c_ref, w_ref, o_ref, *, n_heads):
    kw = w_ref.shape[1]
    nw = w_ref.shape[2]
    for h in range(n_heads):
        o_ref[:, nw * h:nw * (h + 1)] = _dot(c_ref[:, kw * h:kw * (h + 1)], w_ref[h]).astype(o_ref.dtype)


def head_out(c, w):
    m = c.shape[0]
    n_heads, kw, nw = w.shape
    return pl.pallas_call(
        functools.partial(_head_out_kernel, n_heads=n_heads),
        grid=(1,),
        in_specs=[pl.BlockSpec(c.shape, lambda i: (0, 0)), pl.BlockSpec(w.shape, lambda i: (0, 0, 0))],
        out_specs=pl.BlockSpec((m, n_heads * nw), lambda i: (0, 0)),
        out_shape=jax.ShapeDtypeStruct((m, n_heads * nw), BF16),
        compiler_params=_params("arbitrary"),
        name="head_out",
    )(c, w)


def _sort_key(score):
    bits = pltpu.bitcast(score + 0.0, jnp.int32)
    return bits ^ ((bits >> 31) & 0x7FFFFFFF)


def _kth_largest_key(key, k):
    rows = key.shape[0]

    def body(i, t):
        cand = t | jnp.left_shift(jnp.int32(1), 31 - i)
        cnt = jnp.sum(jnp.where(key >= (cand ^ INT_MIN), 1.0, 0.0), axis=-1, keepdims=True)
        return jnp.where(cnt >= k, cand, t)

    t = lax.fori_loop(0, 32, body, jnp.zeros((rows, 1), jnp.int32))
    return t ^ INT_MIN


def _select_topk(key, k, sel_ref, blk):
    rows, n = key.shape
    thr = _kth_largest_key(key, k)
    gt = key > thr
    eq = key == thr
    need = k - jnp.sum(jnp.where(gt, 1.0, 0.0), axis=-1, keepdims=True)
    ri = lax.broadcasted_iota(jnp.int32, (blk, blk), 0)
    ci = lax.broadcasted_iota(jnp.int32, (blk, blk), 1)
    earlier = jnp.where(ri < ci, 1.0, 0.0).astype(BF16)
    seen = jnp.zeros((rows, 1), F32)
    for b in range(n // blk):
        e = jnp.where(eq[:, blk * b:blk * (b + 1)], 1.0, 0.0)
        rank = _dot(e.astype(BF16), earlier) + seen
        seen = seen + jnp.sum(e, axis=-1, keepdims=True)
        take = jnp.where(rank < need, e, 0.0)
        sel_ref[:, blk * b:blk * (b + 1)] = jnp.where(gt[:, blk * b:blk * (b + 1)], 1.0, take)


def _dsa_prompt_kernel(iq_ref, iw_ref, ika_ref, ikb_ref, qa_ref, kv_ref, o_ref, sel_ref, *, tq, h_a, topk):
    i = pl.program_id(1)
    t = kv_ref.shape[0]
    qpos = i * tq + lax.broadcasted_iota(jnp.int32, (tq, 1), 0)
    kpos = lax.broadcasted_iota(jnp.int32, (1, t), 1)
    causal = kpos <= qpos
    ika = ika_ref[...]
    ikb = ikb_ref[...]
    iw = iw_ref[...]
    score = jnp.zeros((tq, t), F32)
    for pr in range(H_IDX // 2):
        q2 = iq_ref[:, LANES * pr:LANES * (pr + 1)]
        score = score + iw[:, 2 * pr:2 * pr + 1] * jnp.maximum(_dot_nt(q2, ika), 0.0)
        score = score + iw[:, 2 * pr + 1:2 * pr + 2] * jnp.maximum(_dot_nt(q2, ikb), 0.0)
    key = _sort_key(jnp.where(causal, score, -jnp.inf))
    _select_topk(key, topk, sel_ref, 256)
    valid = jnp.logical_and(sel_ref[...] > 0.5, causal)
    k = kv_ref[:, 0:HEAD_DIM].astype(BF16)
    v = kv_ref[:, HEAD_DIM:2 * HEAD_DIM].astype(BF16)
    for h in range(h_a):
        q = qa_ref[:, HEAD_DIM * h:HEAD_DIM * (h + 1)]
        s = jnp.where(valid, _dot_nt(q, k) * HEAD_DIM ** -0.5, -jnp.inf)
        e = jnp.exp(s - jnp.max(s, axis=-1, keepdims=True))
        p = e / jnp.sum(e, axis=-1, keepdims=True)
        o_ref[:, HEAD_DIM * h:HEAD_DIM * (h + 1)] = _dot(p.astype(BF16), v).astype(o_ref.dtype)


def dsa_prompt(iq, iw, ika, ikb, qa, kv, batch, t, h_a, topk):
    tq = _pick(t, 128, 16)
    nq = t // tq
    qrow = lambda w: pl.BlockSpec((tq, w), lambda b, i: (b * nq + i, 0))
    krow = lambda w: pl.BlockSpec((t, w), lambda b, i: (b, 0))
    return pl.pallas_call(
        functools.partial(_dsa_prompt_kernel, tq=tq, h_a=h_a, topk=topk),
        grid=(batch, nq),
        in_specs=[qrow(H_IDX * D_IDX), qrow(LANES), krow(LANES), krow(LANES), qrow(h_a * HEAD_DIM),
                  krow(2 * HEAD_DIM)],
        out_specs=qrow(h_a * HEAD_DIM),
        out_shape=jax.ShapeDtypeStruct((batch * t, h_a * HEAD_DIM), BF16),
        scratch_shapes=[pltpu.VMEM((tq, t), F32)],
        compiler_params=_params("parallel", "parallel"),
        name="dsa_prompt",
    )(iq, iw, ika, ikb, qa, kv)


def _mla_prompt_kernel(q_ref, lat_ref, wuv_ref, o_ref, *, tq, h_b):
    i = pl.program_id(1)
    t = lat_ref.shape[0]
    qpos = i * tq + lax.broadcasted_iota(jnp.int32, (tq, 1), 0)
    kpos = lax.broadcasted_iota(jnp.int32, (1, t), 1)
    causal = kpos <= qpos
    c = lat_ref[:, 0:KV_RANK].astype(BF16)
    kr = lat_ref[:, KV_RANK:KV_RANK + D_ROPE_B].astype(BF16)
    qw = KV_RANK + LANES
    for h in range(h_b):
        qn = q_ref[:, qw * h:qw * h + KV_RANK]
        qr = q_ref[:, qw * h + KV_RANK:qw * h + KV_RANK + D_ROPE_B]
        s = (_dot_nt(qn, c) + _dot_nt(qr, kr)) * (D_NOPE + D_ROPE_B) ** -0.5
        s = jnp.where(causal, s, -jnp.inf)
        e = jnp.exp(s - jnp.max(s, axis=-1, keepdims=True))
        p = e / jnp.sum(e, axis=-1, keepdims=True)
        ctx = _dot(p.astype(BF16), c)
        o_ref[:, D_VB * h:D_VB * (h + 1)] = _dot(ctx.astype(BF16), wuv_ref[h]).astype(o_ref.dtype)


def mla_prompt(qlat, lat, wuv, batch, t, h_b):
    tq = _pick(t, 128, 16)
    nq = t // tq
    return pl.pallas_call(
        functools.partial(_mla_prompt_kernel, tq=tq, h_b=h_b),
        grid=(batch, nq),
        in_specs=[pl.BlockSpec((tq, qlat.shape[1]), lambda b, i: (b * nq + i, 0)),
                  pl.BlockSpec((t, lat.shape[1]), lambda b, i: (b, 0)),
                  pl.BlockSpec(wuv.shape, lambda b, i: (0, 0, 0))],
        out_specs=pl.BlockSpec((tq, h_b * D_VB), lambda b, i: (b * nq + i, 0)),
        out_shape=jax.ShapeDtypeStruct((batch * t, h_b * D_VB), BF16),
        compiler_params=_params("parallel", "parallel"),
        name="mla_prompt",
    )(qlat, lat, wuv)


def _later_matrix(blk):
    ri = lax.broadcasted_iota(jnp.int32, (blk, blk), 0)
    ci = lax.broadcasted_iota(jnp.int32, (blk, blk), 1)
    return jnp.where(ri > ci, 1.0, 0.0).astype(BF16)


def _stick_block(z, before, later, carry):
    lk = -_softplus(z)
    if before is not None:
        lk = jnp.where(before, lk, 0.0)
    hi = lk.astype(BF16)
    lo = (lk - hi.astype(F32)).astype(BF16)
    tail = _dot(hi, later) + _dot(lo, later) + carry
    a = jnp.exp(z + lk + tail)
    if before is not None:
        a = jnp.where(before, a, 0.0)
    return a, carry + jnp.sum(lk, axis=-1, keepdims=True)


def _sb_prompt_kernel(q_ref, k_ref, v_ref, o_ref, *, tq, heads, blk):
    i = pl.program_id(2)
    later = _later_matrix(blk)
    qpos = i * tq + lax.broadcasted_iota(jnp.int32, (tq, 1), 0)
    lane = lax.broadcasted_iota(jnp.int32, (1, blk), 1)
    n_blocks = (i * tq + tq - 1) // blk + 1
    for h in range(heads):
        q = q_ref[:, HEAD_DIM * h:HEAD_DIM * (h + 1)].astype(BF16)

        def body(n, state):
            carry, acc = state
            start = pl.multiple_of((n_blocks - 1 - n) * blk, blk)
            k = k_ref[pl.ds(start, blk), :].astype(BF16)
            v = v_ref[pl.ds(start, blk), :].astype(BF16)
            z = _dot_nt(q, k) * HEAD_DIM ** -0.5
            a, carry = _stick_block(z, (start + lane) < qpos, later, carry)
            return carry, acc + _dot(a.astype(BF16), v)

        _, acc = lax.fori_loop(0, n_blocks, body, (jnp.zeros((tq, 1), F32), jnp.zeros((tq, HEAD_DIM), F32)))
        o_ref[:, HEAD_DIM * h:HEAD_DIM * (h + 1)] = acc.astype(o_ref.dtype)


def sb_prompt(proj, batch, t, h_c):
    tq = _pick(t, 128, 16)
    nq = t // tq
    heads = h_c // KVH_C
    blk = _pick(t, 256, LANES)
    qcols = heads * HEAD_DIM
    kcol0 = h_c
    return pl.pallas_call(
        functools.partial(_sb_prompt_kernel, tq=tq, heads=heads, blk=blk),
        grid=(batch, KVH_C, nq),
        in_specs=[pl.BlockSpec((tq, qcols), lambda b, g, i: (b * nq + i, g)),
                  pl.BlockSpec((t, HEAD_DIM), lambda b, g, i: (b, kcol0 + g)),
                  pl.BlockSpec((t, HEAD_DIM), lambda b, g, i: (b, kcol0 + KVH_C + g))],
        out_specs=pl.BlockSpec((tq, qcols), lambda b, g, i: (b * nq + i, g)),
        out_shape=jax.ShapeDtypeStruct((batch * t, h_c * HEAD_DIM), BF16),
        compiler_params=_params("parallel", "parallel", "parallel"),
        name="sb_prompt",
    )(proj, proj, proj)


def _page_specs(rows, cols, layer, pages_per_step, n_steps, reverse):
    specs = []
    for r in range(pages_per_step):
        if reverse:
            imap = lambda b, j, pt, r=r: (layer, pt[b, (n_steps - 1 - j) * pages_per_step + r], 0, 0)
        else:
            imap = lambda b, j, pt, r=r: (layer, pt[b, j * pages_per_step + r], 0, 0)
        specs.append(pl.BlockSpec((None, None, rows, cols), imap))
    return specs


def _cat_lanes(refs, lo, hi):
    return jnp.concatenate([r[lo:hi, :] for r in refs], axis=1).astype(BF16)


def _cat_rows(refs, first, stride):
    return jnp.concatenate([r[pl.ds(first, PAGE_SIZE, stride=stride), :] for r in refs], axis=0).astype(BF16)


def _idx_decode_kernel(pt_ref, iq_ref, iw_ref, new_ref, *rest, n_pg):
    pages = rest[:n_pg]
    score_ref, snew_ref = rest[n_pg:]
    iq = iq_ref[...]
    iw = iw_ref[...]
    keys_t = _cat_lanes(pages, 0, D_IDX)
    score_ref[...] = jnp.sum(iw * jnp.maximum(_dot(iq, keys_t), 0.0), axis=0, keepdims=True)
    knew_t = new_ref[...].astype(BF16)
    snew_ref[...] = jnp.sum(iw * jnp.maximum(_dot(iq, knew_t), 0.0), axis=0, keepdims=True)


def idx_decode(page_table, iq, iw, new_page, pool, layer):
    b, n_pages = page_table.shape
    n_pg = _pick(n_pages, 16, 1)
    n_steps = n_pages // n_pg
    grid_spec = pltpu.PrefetchScalarGridSpec(
        num_scalar_prefetch=1,
        grid=(b, n_steps),
        in_specs=[pl.BlockSpec((None, H_IDX, D_IDX), lambda b_, j, pt: (b_, 0, 0)),
                  pl.BlockSpec((None, H_IDX, 1), lambda b_, j, pt: (b_, 0, 0)),
                  pl.BlockSpec((None, D_IDX, PAGE_SIZE), lambda b_, j, pt: (b_, 0, 0))]
        + _page_specs(D_IDX, PAGE_SIZE, layer, n_pg, n_steps, False),
        out_specs=[pl.BlockSpec((None, 1, n_pg * PAGE_SIZE), lambda b_, j, pt: (b_, 0, j)),
                   pl.BlockSpec((None, 1, PAGE_SIZE), lambda b_, j, pt: (b_, 0, 0))],
    )
    return pl.pallas_call(
        functools.partial(_idx_decode_kernel, n_pg=n_pg),
        grid_spec=grid_spec,
        out_shape=[jax.ShapeDtypeStruct((b, 1, n_pages * PAGE_SIZE), F32),
                   jax.ShapeDtypeStruct((b, 1, PAGE_SIZE), F32)],
        compiler_params=_params("parallel", "arbitrary"),
        name="idx_decode",
    )(page_table, iq, iw, new_page, *([pool] * n_pg))


def _select_decode_kernel(score_ref, sel_ref, *, n_valid, topk):
    score = score_ref[...]
    kpos = lax.broadcasted_iota(jnp.int32, (1, score.shape[1]), 1)
    key = _sort_key(jnp.where(kpos < n_valid, score, -jnp.inf))
    _select_topk(key, topk, sel_ref, LANES)


def select_decode(score, n_valid, topk):
    b, n = score.shape
    tb = _pick(b, 32, 8)
    return pl.pallas_call(
        functools.partial(_select_decode_kernel, n_valid=n_valid, topk=topk),
        grid=(b // tb,),
        in_specs=[pl.BlockSpec((tb, n), lambda i: (i, 0))],
        out_specs=pl.BlockSpec((tb, n), lambda i: (i, 0)),
        out_shape=jax.ShapeDtypeStruct((b, n), F32),
        compiler_params=_params("parallel"),
        name="select_decode",
    )(score)


def _online_softmax_step(s, valid, v, m_ref, l_ref, acc_ref, v_feature_major=False):
    m_old = m_ref[...]
    m_new = jnp.maximum(m_old, jnp.max(jnp.where(valid, s, NEG), axis=-1, keepdims=True))
    alpha = jnp.exp(m_old - m_new)
    p = jnp.where(valid, jnp.exp(s - m_new), 0.0)
    pv = _dot_nt(p.astype(BF16), v) if v_feature_major else _dot(p.astype(BF16), v)
    l_ref[...] = alpha * l_ref[...] + jnp.sum(p, axis=-1, keepdims=True)
    acc_ref[...] = alpha * acc_ref[...] + pv
    m_ref[...] = m_new


def _dsa_decode_kernel(pt_ref, q_ref, sel_ref, selnew_ref, new_ref, *rest, n_pg):
    pages = rest[:n_pg]
    o_ref, m_ref, l_ref, acc_ref = rest[n_pg:]
    j = pl.program_id(1)

    @pl.when(j == 0)
    def _():
        m_ref[...] = jnp.full_like(m_ref, NEG)
        l_ref[...] = jnp.zeros_like(l_ref)
        acc_ref[...] = jnp.zeros_like(acc_ref)

    q = q_ref[...]
    scale = HEAD_DIM ** -0.5
    k = _cat_rows(pages, 0, 2)
    v = _cat_rows(pages, 1, 2)
    _online_softmax_step(_dot_nt(q, k) * scale, sel_ref[...] > 0.5, v, m_ref, l_ref, acc_ref)

    @pl.when(j == pl.num_programs(1) - 1)
    def _():
        kn = _cat_rows([new_ref], 0, 2)
        vn = _cat_rows([new_ref], 1, 2)
        _online_softmax_step(_dot_nt(q, kn) * scale, selnew_ref[...] > 0.5, vn, m_ref, l_ref, acc_ref)
        o_ref[...] = (acc_ref[...] / l_ref[...]).astype(o_ref.dtype)


def dsa_decode(page_table, q, sel, new_page, pool, layer):
    b, n_pages = page_table.shape
    h = q.shape[1]
    n_pg = _pick(n_pages, 16, 1)
    n_steps = n_pages // n_pg
    grid_spec = pltpu.PrefetchScalarGridSpec(
        num_scalar_prefetch=1,
        grid=(b, n_steps),
        in_specs=[pl.BlockSpec((None, h, HEAD_DIM), lambda b_, j, pt: (b_, 0, 0)),
                  pl.BlockSpec((None, 1, n_pg * PAGE_SIZE), lambda b_, j, pt: (b_, 0, j)),
                  pl.BlockSpec((None, 1, PAGE_SIZE), lambda b_, j, pt: (b_, 0, n_pages)),
                  pl.BlockSpec((None, 2 * PAGE_SIZE, HEAD_DIM), lambda b_, j, pt: (b_, 0, 0))]
        + _page_specs(2 * PAGE_SIZE, HEAD_DIM, layer, n_pg, n_steps, False),
        out_specs=pl.BlockSpec((None, h, HEAD_DIM), lambda b_, j, pt: (b_, 0, 0)),
        scratch_shapes=[pltpu.VMEM((h, 1), F32), pltpu.VMEM((h, 1), F32), pltpu.VMEM((h, HEAD_DIM), F32)],
    )
    return pl.pallas_call(
        functools.partial(_dsa_decode_kernel, n_pg=n_pg),
        grid_spec=grid_spec,
        out_shape=jax.ShapeDtypeStruct((b, h, HEAD_DIM), BF16),
        compiler_params=_params("parallel", "arbitrary"),
        name="dsa_decode",
    )(page_table, q, sel, sel, new_page, *([pool] * n_pg))


def _mla_decode_kernel(pt_ref, q_ref, new_ref, *rest, n_pg):
    pages = rest[:n_pg]
    o_ref, m_ref, l_ref, acc_ref = rest[n_pg:]
    j = pl.program_id(1)

    @pl.when(j == 0)
    def _():
        m_ref[...] = jnp.full_like(m_ref, NEG)
        l_ref[...] = jnp.zeros_like(l_ref)
        acc_ref[...] = jnp.zeros_like(acc_ref)

    qn = q_ref[:, 0:KV_RANK]
    qr = q_ref[:, KV_RANK:KV_RANK + D_ROPE_B]
    scale = (D_NOPE + D_ROPE_B) ** -0.5
    c_t = _cat_lanes(pages, 0, KV_RANK)
    kr_t = _cat_lanes(pages, KV_RANK, KV_RANK + D_ROPE_B)
    s = (_dot(qn, c_t) + _dot(qr, kr_t)) * scale
    _online_softmax_step(s, jnp.full(s.shape, True), c_t, m_ref, l_ref, acc_ref, v_feature_major=True)

    @pl.when(j == pl.num_programs(1) - 1)
    def _():
        cn_t = new_ref[0:KV_RANK, :].astype(BF16)
        krn_t = new_ref[KV_RANK:KV_RANK + D_ROPE_B, :].astype(BF16)
        sn = (_dot(qn, cn_t) + _dot(qr, krn_t)) * scale
        first = lax.broadcasted_iota(jnp.int32, sn.shape, 1) < 1
        _online_softmax_step(sn, first, cn_t, m_ref, l_ref, acc_ref, v_feature_major=True)
        o_ref[...] = (acc_ref[...] / l_ref[...]).astype(o_ref.dtype)


def mla_decode(page_table, q, new_page, pool, layer):
    b, n_pages = page_table.shape
    h = q.shape[1]
    width = KV_RANK + D_ROPE_B
    n_pg = _pick(n_pages, 16, 1)
    n_steps = n_pages // n_pg
    grid_spec = pltpu.PrefetchScalarGridSpec(
        num_scalar_prefetch=1,
        grid=(b, n_steps),
        in_specs=[pl.BlockSpec((None, h, q.shape[2]), lambda b_, j, pt: (b_, 0, 0)),
                  pl.BlockSpec((None, width, PAGE_SIZE), lambda b_, j, pt: (b_, 0, 0))]
        + _page_specs(width, PAGE_SIZE, layer, n_pg, n_steps, False),
        out_specs=pl.BlockSpec((None, h, KV_RANK), lambda b_, j, pt: (b_, 0, 0)),
        scratch_shapes=[pltpu.VMEM((h, 1), F32), pltpu.VMEM((h, 1), F32), pltpu.VMEM((h, KV_RANK), F32)],
    )
    return pl.pallas_call(
        functools.partial(_mla_decode_kernel, n_pg=n_pg),
        grid_spec=grid_spec,
        out_shape=jax.ShapeDtypeStruct((b, h, KV_RANK), BF16),
        compiler_params=_params("parallel", "arbitrary"),
        name="mla_decode",
    )(page_table, q, new_page, *([pool] * n_pg))


def _sb_decode_kernel(pt_ref, q_ref, *rest, n_pg, blk):
    pages = rest[:n_pg]
    o_ref, carry_ref, acc_ref = rest[n_pg:]
    j = pl.program_id(1)

    @pl.when(j == 0)
    def _():
        carry_ref[...] = jnp.zeros_like(carry_ref)
        acc_ref[...] = jnp.zeros_like(acc_ref)

    later = _later_matrix(blk)
    n_keys = n_pg * PAGE_SIZE
    for g in range(KVH_C):
        q = q_ref[g]
        k = _cat_rows(pages, g, 2 * KVH_C)
        v = _cat_rows(pages, KVH_C + g, 2 * KVH_C)
        z = _dot_nt(q, k) * HEAD_DIM ** -0.5
        carry = carry_ref[g]
        acc = acc_ref[g]
        for n in reversed(range(n_keys // blk)):
            a, carry = _stick_block(z[:, blk * n:blk * (n + 1)], None, later, carry)
            acc = acc + _dot(a.astype(BF16), v[blk * n:blk * (n + 1), :])
        carry_ref[g] = carry
        acc_ref[g] = acc

    @pl.when(j == pl.num_programs(1) - 1)
    def _():
        o_ref[...] = acc_ref[...].astype(o_ref.dtype)


def sb_decode(page_table, q, pool, layer):
    b, n_pages = page_table.shape
    heads = q.shape[2]
    n_pg = _pick(n_pages, 16, 1)
    n_steps = n_pages // n_pg
    blk = _pick(n_pg * PAGE_SIZE, 256, LANES)
    grid_spec = pltpu.PrefetchScalarGridSpec(
        num_scalar_prefetch=1,
        grid=(b, n_steps),
        in_specs=[pl.BlockSpec((None, KVH_C, heads, HEAD_DIM), lambda b_, j, pt: (b_, 0, 0, 0))]
        + _page_specs(2 * KVH_C * PAGE_SIZE, HEAD_DIM, layer, n_pg, n_steps, True),
        out_specs=pl.BlockSpec((None, KVH_C, heads, HEAD_DIM), lambda b_, j, pt: (b_, 0, 0, 0)),
        scratch_shapes=[pltpu.VMEM((KVH_C, heads, 1), F32), pltpu.VMEM((KVH_C, heads, HEAD_DIM), F32)],
    )
    return pl.pallas_call(
        functools.partial(_sb_decode_kernel, n_pg=n_pg, blk=blk),
        grid_spec=grid_spec,
        out_shape=jax.ShapeDtypeStruct((b, KVH_C, heads, HEAD_DIM), BF16),
        compiler_params=_params("parallel", "arbitrary"),
        name="sb_decode",
    )(page_table, q, *([pool] * n_pg))


def _pad_cols(w, width):
    return jnp.pad(w, ((0, 0), (0, width - w.shape[1])))


def _pad_row(g, width):
    return jnp.pad(g, (0, width - g.shape[0])).reshape(1, width)


def _new_page_rows(rows, rows_per_pos):
    b = rows.shape[0]
    r = rows.reshape(b, rows_per_pos, -1)
    return jnp.pad(r, ((0, 0), (0, rows_per_pos * (PAGE_SIZE - 1)), (0, 0)))


def _new_page_feature_major(rows):
    return jnp.pad(rows[:, :, None], ((0, 0), (0, 0), (0, PAGE_SIZE - 1)))


def even_layer(x, tabs, n_prompt, batch, t, page_table, layer, cache_a_kv, cache_a_idx, cache_b_latent,
               e_norm, e_w_in, a_qnorm, a_knorm, idx_knorm, b_qa_norm, b_wq_b, b_kv_norm, b_wkv_b,
               b_qnorm, b_krnorm, e_w_out):
    m, d = x.shape
    h_a = d // 2 // HEAD_DIM
    h_b = h_a
    past_len = page_table.shape[1] * PAGE_SIZE
    dec_b = m - n_prompt

    widths = (h_a * HEAD_DIM, HEAD_DIM, HEAD_DIM, H_IDX * D_IDX, H_IDX, D_IDX, Q_RANK, KV_RANK, D_ROPE_B)
    src = np.cumsum((0,) + (h_a * HEAD_DIM, HEAD_DIM, HEAD_DIM, H_IDX * D_IDX, H_IDX, D_IDX, Q_RANK, KV_RANK))
    padded = tuple(-(-w // LANES) * LANES for w in widths)
    offs = tuple(int(o) for o in np.cumsum((0,) + padded)[:-1])
    total = -(-sum(padded) // 512) * 512
    cols = [_pad_cols(e_w_in[:, int(s):int(s) + w], p) for s, w, p in zip(src, widths, padded)]
    w_in = _pad_cols(jnp.concatenate(cols, axis=1), total).astype(BF16)

    h = rms_norm_rows(x, e_norm)
    proj = matmul(h, w_in)
    gains = (a_qnorm.reshape(1, -1), a_knorm.reshape(1, -1), _pad_row(idx_knorm, LANES),
             b_qa_norm.reshape(1, -1), b_kv_norm.reshape(1, -1), _pad_row(b_krnorm, LANES))
    qa, kv, iq, iw, ik, ika, ikb, qlo, lat = even_post(proj, tabs, gains, h_a, offs)

    wq = b_wq_b.reshape(Q_RANK, h_b, D_NOPE + D_ROPE_B)
    wq_nope = wq[:, :, :D_NOPE].reshape(Q_RANK, h_b * D_NOPE)
    wq_rope = jnp.pad(wq[:, :, D_NOPE:], ((0, 0), (0, 0), (0, LANES - D_ROPE_B))).reshape(Q_RANK, h_b * LANES)
    qb = matmul(qlo, jnp.concatenate([wq_nope, wq_rope], axis=1).astype(BF16))
    wuk = jnp.transpose(b_wkv_b[:, :, :D_NOPE], (1, 2, 0)).astype(BF16)
    wuv = jnp.transpose(b_wkv_b[:, :, D_NOPE:], (1, 0, 2)).astype(BF16)
    qlat = mla_q(qb, tabs, b_qnorm[:D_NOPE].reshape(1, -1), _pad_row(b_qnorm[D_NOPE:], LANES), wuk, h_b)

    topk_p = min(TOPK_MAX, t // 4)
    oa_p = dsa_prompt(iq, iw, ika, ikb, qa, kv, batch, t, h_a, topk_p)
    ob_p = mla_prompt(qlat, lat, wuv, batch, t, h_b)

    topk_s = min(TOPK_MAX, (past_len + 1) // 4)
    pool_kv = cache_a_kv.reshape(cache_a_kv.shape[:2] + (2 * PAGE_SIZE, HEAD_DIM))
    pool_idx = jnp.swapaxes(cache_a_idx, 2, 3)
    pool_lat = jnp.swapaxes(cache_b_latent, 2, 3)
    iq_s = iq[n_prompt:].reshape(dec_b, H_IDX, D_IDX)
    iw_s = iw[n_prompt:, :H_IDX].reshape(dec_b, H_IDX, 1)
    sc_old, sc_new = idx_decode(page_table, iq_s, iw_s, _new_page_feature_major(ik[n_prompt:]), pool_idx, layer)
    sel = select_decode(jnp.concatenate([sc_old[:, 0], sc_new[:, 0]], axis=1), past_len + 1, topk_s)
    oa_s = dsa_decode(page_table, qa[n_prompt:].reshape(dec_b, h_a, HEAD_DIM), sel[:, None, :],
                      _new_page_rows(kv[n_prompt:], 2), pool_kv, layer)
    ctx_s = mla_decode(page_table, qlat[n_prompt:].reshape(dec_b, h_b, KV_RANK + LANES),
                       _new_page_feature_major(lat[n_prompt:]), pool_lat, layer)
    ob_s = head_out(ctx_s.reshape(dec_b, h_b * KV_RANK), wuv)

    mix_in = jnp.concatenate([jnp.concatenate([oa_p, ob_p], axis=1),
                              jnp.concatenate([oa_s.reshape(dec_b, -1), ob_s], axis=1)], axis=0)
    x = matmul_residual(mix_in, e_w_out.astype(BF16), x, 1.0)
    return x, kv, ik, lat


def odd_layer(x, n_prompt, batch, t, page_table, layer, cache_c_kv, o_norm, o_w_in, o_w_out):
    m, d = x.shape
    h_c = d // HEAD_DIM
    dec_b = m - n_prompt
    h = rms_norm_rows(x, o_norm)
    proj = matmul(h, o_w_in.astype(BF16))
    o_p = sb_prompt(proj, batch, t, h_c)
    q_s = proj[n_prompt:, :h_c * HEAD_DIM].astype(BF16).reshape(dec_b, KVH_C, h_c // KVH_C, HEAD_DIM)
    pool = cache_c_kv.reshape(cache_c_kv.shape[:2] + (2 * KVH_C * PAGE_SIZE, HEAD_DIM))
    o_s = sb_decode(page_table, q_s, pool, layer)
    mix_in = jnp.concatenate([o_p, o_s.reshape(dec_b, -1)], axis=0)
    x = matmul_residual(mix_in, o_w_out.astype(BF16), x, 1.0)
    return x, proj[:, h_c * HEAD_DIM:]


def kernel(x_prompt, x_sample, cache_a_kv, cache_a_idx, cache_b_latent, cache_c_kv, page_table, ffn1_norm, ffn1_w_in, ffn1_w_out, ffn2_norm, ffn2_w_in, ffn2_w_out, e_norm, e_w_in, e_a_qnorm, e_a_knorm, e_idx_knorm, e_b_qa_norm, e_b_wq_b, e_b_kv_norm, e_b_wkv_b, e_b_qnorm, e_b_krnorm, e_w_out, o_norm, o_w_in, o_w_out):
    batch, t, d = x_prompt.shape
    dec_b, dec_t, _ = x_sample.shape
    assert dec_t == 1, "one new token per sampled sequence"
    depth = ffn1_norm.shape[0]
    n_prompt = batch * t
    past_len = page_table.shape[1] * PAGE_SIZE
    x = jnp.concatenate([x_prompt.reshape(n_prompt, d), x_sample.reshape(dec_b, d)], axis=0)

    pos = jnp.concatenate([jnp.tile(jnp.arange(t, dtype=F32), batch),
                           jnp.full((dec_b,), past_len, F32)]).reshape(-1, 1)
    rows = jnp.stack([_rope_rows(HEAD_DIM // 4, HEAD_DIM), _rope_rows(D_IDX // 4, D_IDX),
                      _rope_rows(D_ROPE_B, LANES)])
    tabs = rope_tables(pos, rows)

    akv, aidx, blat, ckv = [], [], [], []
    for layer in range(depth):
        x = ffn_half_step(x, ffn1_norm[layer], ffn1_w_in[layer].astype(BF16), ffn1_w_out[layer].astype(BF16))
        if layer % 2 == 0:
            e = layer // 2
            x, kv, ik, lat = even_layer(
                x, tabs, n_prompt, batch, t, page_table, e, cache_a_kv, cache_a_idx, cache_b_latent,
                e_norm[e], e_w_in[e], e_a_qnorm[e], e_a_knorm[e], e_idx_knorm[e], e_b_qa_norm[e], e_b_wq_b[e],
                e_b_kv_norm[e], e_b_wkv_b[e], e_b_qnorm[e], e_b_krnorm[e], e_w_out[e])
            akv.append(kv)
            aidx.append(ik)
            blat.append(lat)
        else:
            o = layer // 2
            x, kvc = odd_layer(x, n_prompt, batch, t, page_table, o, cache_c_kv, o_norm[o], o_w_in[o], o_w_out[o])
            ckv.append(kvc)
        x = ffn_half_step(x, ffn2_norm[layer], ffn2_w_in[layer].astype(BF16), ffn2_w_out[layer].astype(BF16))

    def split(rows_list, tail):
        a = jnp.stack(rows_list)
        return (a[:, :n_prompt].reshape((len(rows_list), batch, t) + tail),
                a[:, n_prompt:].reshape((len(rows_list), dec_b, dec_t) + tail))

    akv_p, akv_s = split(akv, (2, 1, HEAD_DIM))
    aidx_p, aidx_s = split(aidx, (D_IDX,))
    blat_p, blat_s = split(blat, (KV_RANK + D_ROPE_B,))
    ckv_p, ckv_s = split(ckv, (2, KVH_C, HEAD_DIM))
    return (x[:n_prompt].reshape(batch, t, d), x[n_prompt:].reshape(dec_b, dec_t, d),
            akv_p, akv_s, aidx_p, aidx_s, blat_p, blat_s, ckv_p, ckv_s)
```

```python
import functools

import numpy as np
import jax
import jax.numpy as jnp
from jax import lax
from jax.experimental import pallas as pl
from jax.experimental.pallas import tpu as pltpu

HEAD_DIM = 128
ROPE_THETA = 500000.0
EPS = 1e-6
PAGE_SIZE = 128
H_IDX = 32
D_IDX = 64
TOPK_MAX = 256
Q_RANK = 768
KV_RANK = 256
D_NOPE = 128
D_ROPE_B = 64
D_VB = 128
KVH_C = 2
LANES = 128
V7X_VMEM_LIMIT = 56 * 1024 * 1024
NEG = -1e30
INT_MIN = -(2 ** 31)
NEG_INF_KEY = INT_MIN + 0x7FFFFF

BF16 = jnp.bfloat16
F32 = jnp.float32


def _pick(n, target, mult):
    best = None
    for d in range(mult, min(n, target) + 1, mult):
        if n % d == 0:
            best = d
    assert best is not None, (n, target, mult)
    return best


def _params(*sem):
    return pltpu.CompilerParams(dimension_semantics=sem, vmem_limit_bytes=V7X_VMEM_LIMIT)


def _dot(a, b):
    return jnp.dot(a, b, preferred_element_type=F32)


def _dot_nt(a, b):
    return lax.dot_general(a, b, (((1,), (1,)), ((), ())), preferred_element_type=F32)


def _softplus(z):
    return jnp.maximum(z, 0.0) + jnp.log(1.0 + jnp.exp(-jnp.abs(z)))


def _norm_kernel(x_ref, g_ref, o_ref):
    x = x_ref[...]
    y = x * lax.rsqrt(jnp.mean(x * x, axis=-1, keepdims=True) + EPS)
    o_ref[...] = (y * g_ref[...]).astype(o_ref.dtype)


def rms_norm_rows(x, g):
    m, d = x.shape
    tm = _pick(m, 512, 16)
    return pl.pallas_call(
        _norm_kernel,
        grid=(m // tm,),
        in_specs=[pl.BlockSpec((tm, d), lambda i: (i, 0)), pl.BlockSpec((1, d), lambda i: (0, 0))],
        out_specs=pl.BlockSpec((tm, d), lambda i: (i, 0)),
        out_shape=jax.ShapeDtypeStruct((m, d), BF16),
        compiler_params=_params("parallel"),
        name="rms_norm_rows",
    )(x, g.reshape(1, d))


def _mm_kernel(a_ref, w_ref, o_ref):
    o_ref[...] = _dot(a_ref[...], w_ref[...]).astype(o_ref.dtype)


def matmul(a, w, out_dtype=F32, tm_target=1664, tn_target=512):
    m, k = a.shape
    n = w.shape[1]
    tm = _pick(m, tm_target, 16)
    tn = _pick(n, tn_target, LANES)
    return pl.pallas_call(
        _mm_kernel,
        grid=(m // tm, n // tn),
        in_specs=[pl.BlockSpec((tm, k), lambda i, j: (i, 0)), pl.BlockSpec((k, tn), lambda i, j: (0, j))],
        out_specs=pl.BlockSpec((tm, tn), lambda i, j: (i, j)),
        out_shape=jax.ShapeDtypeStruct((m, n), out_dtype),
        compiler_params=_params("parallel", "parallel"),
        name="matmul",
    )(a, w)


def _mm_res_kernel(a_ref, w_ref, r_ref, o_ref, *, scale):
    o_ref[...] = r_ref[...] + scale * _dot(a_ref[...], w_ref[...])


def matmul_residual(a, w, res, scale, tm_target=416, tn_target=512, row_start=0, n_rows=None):
    k = a.shape[1]
    n = w.shape[1]
    n_rows = a.shape[0] if n_rows is None else n_rows
    tm = _pick(int(np.gcd(row_start, n_rows)), tm_target, 16)
    tn = _pick(n, tn_target, LANES)
    i0 = row_start // tm
    return pl.pallas_call(
        functools.partial(_mm_res_kernel, scale=scale),
        grid=(n_rows // tm, n // tn),
        in_specs=[pl.BlockSpec((tm, k), lambda i, j: (i + i0, 0)), pl.BlockSpec((k, tn), lambda i, j: (0, j)),
                  pl.BlockSpec((tm, tn), lambda i, j: (i + i0, j))],
        out_specs=pl.BlockSpec((tm, tn), lambda i, j: (i, j)),
        out_shape=jax.ShapeDtypeStruct((n_rows, n), F32),
        compiler_params=_params("parallel", "parallel"),
        name="matmul_residual",
    )(a, w, res)


def _ffn_in_kernel(a_ref, wg_ref, wu_ref, o_ref):
    a = a_ref[...]
    g = _dot(a, wg_ref[...])
    u = _dot(a, wu_ref[...])
    o_ref[...] = (g * jax.nn.sigmoid(g) * u).astype(o_ref.dtype)


def ffn_in(h, w_in, tm_target=1664, tn_target=256):
    m, k = h.shape
    f = w_in.shape[1] // 2
    tm = _pick(m, tm_target, 16)
    tn = _pick(f, tn_target, LANES)
    nj = f // tn
    return pl.pallas_call(
        _ffn_in_kernel,
        grid=(m // tm, nj),
        in_specs=[pl.BlockSpec((tm, k), lambda i, j: (i, 0)), pl.BlockSpec((k, tn), lambda i, j: (0, j)),
                  pl.BlockSpec((k, tn), lambda i, j: (0, j + nj))],
        out_specs=pl.BlockSpec((tm, tn), lambda i, j: (i, j)),
        out_shape=jax.ShapeDtypeStruct((m, f), BF16),
        compiler_params=_params("parallel", "parallel"),
        name="ffn_in",
    )(h, w_in, w_in)


def ffn_half_step(x, g, w_in, w_out, split_at=None):
    h = rms_norm_rows(x, g)
    act = ffn_in(h, w_in)
    if split_at is None:
        return matmul_residual(act, w_out, x, 0.5)
    head = matmul_residual(act, w_out, x, 0.5, tm_target=512, tn_target=256, n_rows=split_at)
    tail = matmul_residual(act, w_out, x, 0.5, row_start=split_at, n_rows=x.shape[0] - split_at)
    return head, tail


def _rope_rows(rot, width):
    half = rot // 2
    j = np.arange(LANES) % width
    in_rot = (j < rot).astype(np.float32)
    take_lo = ((j >= half) & (j < rot)).astype(np.float32)
    take_hi = (j < half).astype(np.float32)
    inv = ROPE_THETA ** (-jnp.arange(half, dtype=F32) / half)
    inv_row = jnp.where(jnp.asarray(j < rot), inv[j % half], 0.0).astype(F32)
    return jnp.stack([inv_row, jnp.asarray(in_rot), jnp.asarray(take_lo), jnp.asarray(take_hi)])


def _rope_table_kernel(pos_ref, rows_ref, o_ref):
    pos = pos_ref[...]
    for f in range(rows_ref.shape[0]):
        rows = rows_ref[f]
        ang = pos * rows[0:1]
        cos = jnp.cos(ang)
        sin = jnp.sin(ang)
        o_ref[f, 0] = jnp.where(rows[1:2] > 0.5, cos, 1.0)
        o_ref[f, 1] = rows[2:3] * sin
        o_ref[f, 2] = -rows[3:4] * sin


def rope_tables(pos, rows):
    m = pos.shape[0]
    nf = rows.shape[0]
    tm = _pick(m, 512, 8)
    return pl.pallas_call(
        _rope_table_kernel,
        grid=(m // tm,),
        in_specs=[pl.BlockSpec((tm, 1), lambda i: (i, 0)), pl.BlockSpec((nf, 4, LANES), lambda i: (0, 0, 0))],
        out_specs=pl.BlockSpec((nf, 3, tm, LANES), lambda i: (0, 0, i, 0)),
        out_shape=jax.ShapeDtypeStruct((nf, 3, m, LANES), F32),
        compiler_params=_params("parallel"),
        name="rope_tables",
    )(pos, rows)


def _rope(x, tab, half):
    return x * tab[0] + pltpu.roll(x, half, 1) * tab[1] + pltpu.roll(x, LANES - half, 1) * tab[2]


def _head_norm(x, g, n):
    return x * lax.rsqrt(jnp.sum(x * x, axis=-1, keepdims=True) / n + EPS) * g


def _even_post_kernel(p_ref, tab_ref, gq_ref, gk_ref, gik_ref, gqa_ref, gkv_ref, gkr_ref,
                      qa_o, kv_o, iq_o, iw_o, ik_o, ika_o, ikb_o, qlo_o, lat_o, *, h_a, offs):
    o_qa, o_ka, o_va, o_iq, o_iw, o_ik, o_qlo, o_c, o_kr = offs
    tab_a = tab_ref[0]
    tab_i = tab_ref[1]
    tab_b = tab_ref[2]
    rot_a = HEAD_DIM // 4
    rot_i = D_IDX // 4
    for h in range(h_a):
        x = p_ref[:, o_qa + LANES * h:o_qa + LANES * (h + 1)]
        y = _rope(_head_norm(x, gq_ref[...], HEAD_DIM), tab_a, rot_a // 2)
        qa_o[:, LANES * h:LANES * (h + 1)] = y.astype(qa_o.dtype)
    ka = _rope(_head_norm(p_ref[:, o_ka:o_ka + LANES], gk_ref[...], HEAD_DIM), tab_a, rot_a // 2)
    kv_o[:, 0:LANES] = ka
    kv_o[:, LANES:2 * LANES] = p_ref[:, o_va:o_va + LANES]
    for b in range(H_IDX * D_IDX // LANES):
        x = p_ref[:, o_iq + LANES * b:o_iq + LANES * (b + 1)]
        iq_o[:, LANES * b:LANES * (b + 1)] = _rope(x, tab_i, rot_i // 2).astype(iq_o.dtype)
    iw_o[...] = p_ref[:, o_iw:o_iw + LANES] * (H_IDX * D_IDX) ** -0.5
    ik = _rope(_head_norm(p_ref[:, o_ik:o_ik + LANES], gik_ref[...], D_IDX), tab_i, rot_i // 2)
    ik_o[...] = ik[:, :D_IDX]
    ika_o[...] = ik.astype(ika_o.dtype)
    ikb_o[...] = pltpu.roll(ik, D_IDX, 1).astype(ikb_o.dtype)
    qlo = p_ref[:, o_qlo:o_qlo + Q_RANK]
    qlo_o[...] = _head_norm(qlo, gqa_ref[...], Q_RANK).astype(qlo_o.dtype)
    lat_o[:, 0:KV_RANK] = _head_norm(p_ref[:, o_c:o_c + KV_RANK], gkv_ref[...], KV_RANK)
    kr = _rope(_head_norm(p_ref[:, o_kr:o_kr + LANES], gkr_ref[...], D_ROPE_B), tab_b, D_ROPE_B // 2)
    lat_o[:, KV_RANK:KV_RANK + D_ROPE_B] = kr[:, :D_ROPE_B]


def even_post(proj, tabs, gains, h_a, offs):
    m, npj = proj.shape
    tm = _pick(m, 320, 16)
    row = lambda w: pl.BlockSpec((tm, w), lambda i: (i, 0))
    full = lambda a: pl.BlockSpec(a.shape, lambda i: (0,) * a.ndim)
    out_shape = [
        jax.ShapeDtypeStruct((m, h_a * HEAD_DIM), BF16),
        jax.ShapeDtypeStruct((m, 2 * HEAD_DIM), F32),
        jax.ShapeDtypeStruct((m, H_IDX * D_IDX), BF16),
        jax.ShapeDtypeStruct((m, LANES), F32),
        jax.ShapeDtypeStruct((m, D_IDX), F32),
        jax.ShapeDtypeStruct((m, LANES), BF16),
        jax.ShapeDtypeStruct((m, LANES), BF16),
        jax.ShapeDtypeStruct((m, Q_RANK), BF16),
        jax.ShapeDtypeStruct((m, KV_RANK + D_ROPE_B), F32),
    ]
    return pl.pallas_call(
        functools.partial(_even_post_kernel, h_a=h_a, offs=offs),
        grid=(m // tm,),
        in_specs=[row(npj), pl.BlockSpec((3, 3, tm, LANES), lambda i: (0, 0, i, 0))] + [full(g) for g in gains],
        out_specs=[row(s.shape[1]) for s in out_shape],
        out_shape=out_shape,
        compiler_params=_params("parallel"),
        name="even_post",
    )(proj, tabs, *gains)


def _mla_q_kernel(qb_ref, tab_ref, gn_ref, gr_ref, wuk_ref, o_ref, *, h_b):
    tab_b = tab_ref[0]
    qw = KV_RANK + LANES
    for h in range(h_b):
        xn = qb_ref[:, LANES * h:LANES * (h + 1)]
        xr = qb_ref[:, LANES * (h_b + h):LANES * (h_b + h + 1)]
        ss = jnp.sum(xn * xn, axis=-1, keepdims=True) + jnp.sum(xr * xr, axis=-1, keepdims=True)
        r = lax.rsqrt(ss / (D_NOPE + D_ROPE_B) + EPS)
        yn = xn * r * gn_ref[...]
        yr = _rope(xr * r * gr_ref[...], tab_b, D_ROPE_B // 2)
        o_ref[:, qw * h:qw * h + KV_RANK] = _dot(yn.astype(BF16), wuk_ref[h]).astype(o_ref.dtype)
        o_ref[:, qw * h + KV_RANK:qw * (h + 1)] = yr.astype(o_ref.dtype)


def mla_q(qb, tabs, gn, gr, wuk, h_b):
    m = qb.shape[0]
    tm = _pick(m, 320, 16)
    qw = KV_RANK + LANES
    return pl.pallas_call(
        functools.partial(_mla_q_kernel, h_b=h_b),
        grid=(m // tm,),
        in_specs=[pl.BlockSpec((tm, qb.shape[1]), lambda i: (i, 0)),
                  pl.BlockSpec((1, 3, tm, LANES), lambda i: (2, 0, i, 0)),
                  pl.BlockSpec((1, LANES), lambda i: (0, 0)), pl.BlockSpec((1, LANES), lambda i: (0, 0)),
                  pl.BlockSpec(wuk.shape, lambda i: (0, 0, 0))],
        out_specs=pl.BlockSpec((tm, h_b * qw), lambda i: (i, 0)),
        out_shape=jax.ShapeDtypeStruct((m, h_b * qw), BF16),
        compiler_params=_params("parallel"),
        name="mla_q",
    )(qb, tabs, gn, gr, wuk)


def _head_out_kernel(c_ref, w_ref, o_ref, *, n_heads):
    kw = w_ref.shape[1]
    nw = w_ref.shape[2]
    for h in range(n_heads):
        o_ref[:, nw * h:nw * (h + 1)] = _dot(c_ref[:, kw * h:kw * (h + 1)], w_ref[h]).astype(o_ref.dtype)


def head_out(c, w):
    m = c.shape[0]
    n_heads, kw, nw = w.shape
    return pl.pallas_call(
        functools.partial(_head_out_kernel, n_heads=n_heads),
        grid=(1,),
        in_specs=[pl.BlockSpec(c.shape, lambda i: (0, 0)), pl.BlockSpec(w.shape, lambda i: (0, 0, 0))],
        out_specs=pl.BlockSpec((m, n_heads * nw), lambda i: (0, 0)),
        out_shape=jax.ShapeDtypeStruct((m, n_heads * nw), BF16),
        compiler_params=_params("arbitrary"),
        name="head_out",
    )(c, w)


def _sort_key(score):
    bits = pltpu.bitcast(score + 0.0, jnp.int32)
    return bits ^ ((bits >> 31) & 0x7FFFFFFF)


def _count(keys, pred):
    return sum(jnp.sum(jnp.where(pred(key), 1.0, 0.0), axis=-1, keepdims=True) for key in keys)


def _kth_largest_key(keys, k):
    rows = keys[0].shape[0]

    def body(i, t):
        cand = t | jnp.left_shift(jnp.int32(1), 31 - i)
        cnt = _count(keys, lambda key: key >= (cand ^ INT_MIN))
        return jnp.where(cnt >= k, cand, t)

    t = lax.fori_loop(0, 32, body, jnp.zeros((rows, 1), jnp.int32))
    return t ^ INT_MIN


def _select_topk(keys, k, write, blk):
    rows = keys[0].shape[0]
    thr = _kth_largest_key(keys, k)
    need = k - _count(keys, lambda key: key > thr)
    ri = lax.broadcasted_iota(jnp.int32, (blk, blk), 0)
    ci = lax.broadcasted_iota(jnp.int32, (blk, blk), 1)
    earlier = jnp.where(ri < ci, 1.0, 0.0).astype(BF16)
    seen = jnp.zeros((rows, 1), F32)
    for c, key in enumerate(keys):
        for b in range(key.shape[1] // blk):
            kb = key[:, blk * b:blk * (b + 1)]
            e = jnp.where(kb == thr, 1.0, 0.0)
            rank = _dot(e.astype(BF16), earlier) + seen
            seen = seen + jnp.sum(e, axis=-1, keepdims=True)
            take = jnp.where(rank < need, e, 0.0)
            write(c, blk * b, blk * (b + 1), jnp.where(kb > thr, 1.0, take))


def _stack_heads(src_ref, dst_ref, heads, tq, width):
    for h in range(heads):
        dst_ref[tq * h:tq * (h + 1), :] = src_ref[:, width * h:width * (h + 1)].astype(dst_ref.dtype)


def _tile_rows(x, reps):
    return jnp.concatenate([x] * reps, axis=0)


def _dsa_prompt_kernel(iq_ref, iw_ref, ika_ref, ikb_ref, qa_ref, kv_ref, o_ref,
                       iqs_ref, qas_ref, key_ref, sel_ref, m_ref, l_ref, acc_ref, *, tq, h_a, topk, ck):
    i = pl.program_id(1)
    n_chunks = kv_ref.shape[0] // ck
    n_pairs = H_IDX // 2
    _stack_heads(iq_ref, iqs_ref, n_pairs, tq, LANES)
    _stack_heads(qa_ref, qas_ref, h_a, tq, HEAD_DIM)
    qpos = i * tq + lax.broadcasted_iota(jnp.int32, (tq, 1), 0)
    lane = lax.broadcasted_iota(jnp.int32, (1, ck), 1)
    last = i * tq + tq - 1
    iw = iw_ref[...]

    for c in range(n_chunks):
        @pl.when(c * ck <= last)
        def _():
            sa = jnp.maximum(_dot_nt(iqs_ref[...], ika_ref[ck * c:ck * (c + 1), :]), 0.0)
            sb = jnp.maximum(_dot_nt(iqs_ref[...], ikb_ref[ck * c:ck * (c + 1), :]), 0.0)
            score = jnp.zeros((tq, ck), F32)
            for pr in range(n_pairs):
                score = score + iw[:, 2 * pr:2 * pr + 1] * sa[tq * pr:tq * (pr + 1)]
                score = score + iw[:, 2 * pr + 1:2 * pr + 2] * sb[tq * pr:tq * (pr + 1)]
            key_ref[c] = _sort_key(jnp.where((ck * c + lane) <= qpos, score, -jnp.inf))

        @pl.when(c * ck > last)
        def _():
            key_ref[c] = jnp.full((tq, ck), NEG_INF_KEY, jnp.int32)

    def write(c, lo, hi, value):
        sel_ref[c, :, lo:hi] = value

    _select_topk([key_ref[c] for c in range(n_chunks)], topk, write, 256)

    m_ref[...] = jnp.full_like(m_ref, NEG)
    l_ref[...] = jnp.zeros_like(l_ref)
    acc_ref[...] = jnp.zeros_like(acc_ref)
    for c in range(n_chunks):
        @pl.when(c * ck <= last)
        def _():
            k = kv_ref[ck * c:ck * (c + 1), 0:HEAD_DIM].astype(BF16)
            v = kv_ref[ck * c:ck * (c + 1), HEAD_DIM:2 * HEAD_DIM].astype(BF16)
            s = _dot_nt(qas_ref[...], k) * HEAD_DIM ** -0.5
            chosen = jnp.where((ck * c + lane) <= qpos, sel_ref[c], 0.0)
            _online_softmax_step(s, _tile_rows(chosen, h_a) > 0.5, v, m_ref, l_ref, acc_ref)

    out = acc_ref[...] / l_ref[...]
    for h in range(h_a):
        o_ref[:, HEAD_DIM * h:HEAD_DIM * (h + 1)] = out[tq * h:tq * (h + 1)].astype(o_ref.dtype)


def dsa_prompt(iq, iw, ika, ikb, qa, kv, batch, t, h_a, topk):
    tq = _pick(t, 128, 16)
    nq = t // tq
    ck = _pick(t, 512, 256)
    qrow = lambda w: pl.BlockSpec((tq, w), lambda b, i: (b * nq + i, 0))
    krow = lambda w: pl.BlockSpec((t, w), lambda b, i: (b, 0))
    return pl.pallas_call(
        functools.partial(_dsa_prompt_kernel, tq=tq, h_a=h_a, topk=topk, ck=ck),
        grid=(batch, nq),
        in_specs=[qrow(H_IDX * D_IDX), qrow(LANES), krow(LANES), krow(LANES), qrow(h_a * HEAD_DIM),
                  krow(2 * HEAD_DIM)],
        out_specs=qrow(h_a * HEAD_DIM),
        out_shape=jax.ShapeDtypeStruct((batch * t, h_a * HEAD_DIM), BF16),
        scratch_shapes=[pltpu.VMEM((H_IDX // 2 * tq, LANES), BF16), pltpu.VMEM((h_a * tq, HEAD_DIM), BF16),
                        pltpu.VMEM((t // ck, tq, ck), jnp.int32), pltpu.VMEM((t // ck, tq, ck), F32),
                        pltpu.VMEM((h_a * tq, 1), F32), pltpu.VMEM((h_a * tq, 1), F32),
                        pltpu.VMEM((h_a * tq, HEAD_DIM), F32)],
        compiler_params=_params("parallel", "parallel"),
        name="dsa_prompt",
    )(iq, iw, ika, ikb, qa, kv)


def _mla_prompt_kernel(q_ref, lat_ref, wuv_ref, o_ref, qs_ref, *, tq, h_b, ck):
    i = pl.program_id(1)
    rows = h_b * tq
    _stack_heads(q_ref, qs_ref, h_b, tq, KV_RANK + LANES)
    qpos = _tile_rows(i * tq + lax.broadcasted_iota(jnp.int32, (tq, 1), 0), h_b)
    lane = lax.broadcasted_iota(jnp.int32, (1, ck), 1)
    scale = (D_NOPE + D_ROPE_B) ** -0.5

    def body(n, state):
        m_old, l_old, acc = state
        start = pl.multiple_of(n * ck, ck)
        c = lat_ref[pl.ds(start, ck), 0:KV_RANK].astype(BF16)
        kr = lat_ref[pl.ds(start, ck), KV_RANK:KV_RANK + D_ROPE_B].astype(BF16)
        s = (_dot_nt(qs_ref[:, 0:KV_RANK], c) + _dot_nt(qs_ref[:, KV_RANK:KV_RANK + D_ROPE_B], kr)) * scale
        valid = (start + lane) <= qpos
        m_new = jnp.maximum(m_old, jnp.max(jnp.where(valid, s, NEG), axis=-1, keepdims=True))
        alpha = jnp.exp(m_old - m_new)
        p = jnp.where(valid, jnp.exp(s - m_new), 0.0)
        l_new = alpha * l_old + jnp.sum(p, axis=-1, keepdims=True)
        return m_new, l_new, alpha * acc + _dot(p.astype(BF16), c)

    n_chunks = (i * tq + tq - 1) // ck + 1
    init = (jnp.full((rows, 1), NEG, F32), jnp.zeros((rows, 1), F32), jnp.zeros((rows, KV_RANK), F32))
    _, l_fin, acc = lax.fori_loop(0, n_chunks, body, init)
    ctx = (acc / l_fin).astype(BF16)
    for h in range(h_b):
        o_ref[:, D_VB * h:D_VB * (h + 1)] = _dot(ctx[tq * h:tq * (h + 1)], wuv_ref[h]).astype(o_ref.dtype)


def mla_prompt(qlat, lat, wuv, batch, t, h_b):
    tq = _pick(t, 128, 16)
    nq = t // tq
    ck = _pick(t, 512, 256)
    return pl.pallas_call(
        functools.partial(_mla_prompt_kernel, tq=tq, h_b=h_b, ck=ck),
        grid=(batch, nq),
        in_specs=[pl.BlockSpec((tq, qlat.shape[1]), lambda b, i: (b * nq + i, 0)),
                  pl.BlockSpec((t, lat.shape[1]), lambda b, i: (b, 0)),
                  pl.BlockSpec(wuv.shape, lambda b, i: (0, 0, 0))],
        out_specs=pl.BlockSpec((tq, h_b * D_VB), lambda b, i: (b * nq + i, 0)),
        out_shape=jax.ShapeDtypeStruct((batch * t, h_b * D_VB), BF16),
        scratch_shapes=[pltpu.VMEM((h_b * tq, KV_RANK + LANES), BF16)],
        compiler_params=_params("parallel", "parallel"),
        name="mla_prompt",
    )(qlat, lat, wuv)


def _later_matrix(blk):
    ri = lax.broadcasted_iota(jnp.int32, (blk, blk), 0)
    ci = lax.broadcasted_iota(jnp.int32, (blk, blk), 1)
    return jnp.where(ri > ci, 1.0, 0.0).astype(BF16)


def _stick_block(z, before, later, carry):
    lk = -_softplus(z)
    if before is not None:
        lk = jnp.where(before, lk, 0.0)
    hi = lk.astype(BF16)
    lo = (lk - hi.astype(F32)).astype(BF16)
    tail = _dot(hi, later) + _dot(lo, later) + carry
    a = jnp.exp(z + lk + tail)
    if before is not None:
        a = jnp.where(before, a, 0.0)
    return a, carry + jnp.sum(lk, axis=-1, keepdims=True)


def _sb_prompt_kernel(q_ref, k_ref, v_ref, o_ref, qs_ref, *, tq, heads, blk):
    i = pl.program_id(2)
    rows = heads * tq
    later = _later_matrix(blk)
    _stack_heads(q_ref, qs_ref, heads, tq, HEAD_DIM)
    qpos = _tile_rows(i * tq + lax.broadcasted_iota(jnp.int32, (tq, 1), 0), heads)
    lane = lax.broadcasted_iota(jnp.int32, (1, blk), 1)
    n_blocks = (i * tq + tq - 1) // blk + 1

    def body(n, state):
        carry, acc = state
        start = pl.multiple_of((n_blocks - 1 - n) * blk, blk)
        k = k_ref[pl.ds(start, blk), :].astype(BF16)
        v = v_ref[pl.ds(start, blk), :].astype(BF16)
        z = _dot_nt(qs_ref[...], k) * HEAD_DIM ** -0.5
        a, carry = _stick_block(z, (start + lane) < qpos, later, carry)
        return carry, acc + _dot(a.astype(BF16), v)

    _, acc = lax.fori_loop(0, n_blocks, body, (jnp.zeros((rows, 1), F32), jnp.zeros((rows, HEAD_DIM), F32)))
    for h in range(heads):
        o_ref[:, HEAD_DIM * h:HEAD_DIM * (h + 1)] = acc[tq * h:tq * (h + 1)].astype(o_ref.dtype)


def sb_prompt(proj, batch, t, h_c):
    tq = _pick(t, 128, 16)
    nq = t // tq
    heads = h_c // KVH_C
    blk = _pick(t, 256, LANES)
    qcols = heads * HEAD_DIM
    kcol0 = h_c
    return pl.pallas_call(
        functools.partial(_sb_prompt_kernel, tq=tq, heads=heads, blk=blk),
        grid=(batch, KVH_C, nq),
        in_specs=[pl.BlockSpec((tq, qcols), lambda b, g, i: (b * nq + i, g)),
                  pl.BlockSpec((t, HEAD_DIM), lambda b, g, i: (b, kcol0 + g)),
                  pl.BlockSpec((t, HEAD_DIM), lambda b, g, i: (b, kcol0 + KVH_C + g))],
        out_specs=pl.BlockSpec((tq, qcols), lambda b, g, i: (b * nq + i, g)),
        out_shape=jax.ShapeDtypeStruct((batch * t, h_c * HEAD_DIM), BF16),
        scratch_shapes=[pltpu.VMEM((heads * tq, HEAD_DIM), BF16)],
        compiler_params=_params("parallel", "parallel", "parallel"),
        name="sb_prompt",
    )(proj, proj, proj)


def _page_specs(rows, cols, layer, pages_per_step, n_steps, reverse):
    specs = []
    for r in range(pages_per_step):
        if reverse:
            imap = lambda b, j, pt, r=r: (layer, pt[b, (n_steps - 1 - j) * pages_per_step + r], 0, 0)
        else:
            imap = lambda b, j, pt, r=r: (layer, pt[b, j * pages_per_step + r], 0, 0)
        specs.append(pl.BlockSpec((None, None, rows, cols), imap))
    return specs


def _cat_lanes(refs, lo, hi):
    return jnp.concatenate([r[lo:hi, :] for r in refs], axis=1).astype(BF16)


def _cat_rows(refs, first, stride):
    return jnp.concatenate([r[pl.ds(first, PAGE_SIZE, stride=stride), :] for r in refs], axis=0).astype(BF16)


def _idx_decode_kernel(pt_ref, iq_ref, iw_ref, new_ref, *rest, n_pg):
    pages = rest[:n_pg]
    score_ref, snew_ref = rest[n_pg:]
    iq = iq_ref[...]
    iw = iw_ref[...]
    keys_t = _cat_lanes(pages, 0, D_IDX)
    score_ref[...] = jnp.sum(iw * jnp.maximum(_dot(iq, keys_t), 0.0), axis=0, keepdims=True)
    knew_t = new_ref[...].astype(BF16)
    snew_ref[...] = jnp.sum(iw * jnp.maximum(_dot(iq, knew_t), 0.0), axis=0, keepdims=True)


def idx_decode(page_table, iq, iw, new_page, pool, layer):
    b, n_pages = page_table.shape
    n_pg = _pick(n_pages, 16, 1)
    n_steps = n_pages // n_pg
    grid_spec = pltpu.PrefetchScalarGridSpec(
        num_scalar_prefetch=1,
        grid=(b, n_steps),
        in_specs=[pl.BlockSpec((None, H_IDX, D_IDX), lambda b_, j, pt: (b_, 0, 0)),
                  pl.BlockSpec((None, H_IDX, 1), lambda b_, j, pt: (b_, 0, 0)),
                  pl.BlockSpec((None, D_IDX, PAGE_SIZE), lambda b_, j, pt: (b_, 0, 0))]
        + _page_specs(D_IDX, PAGE_SIZE, layer, n_pg, n_steps, False),
        out_specs=[pl.BlockSpec((None, 1, n_pg * PAGE_SIZE), lambda b_, j, pt: (b_, 0, j)),
                   pl.BlockSpec((None, 1, PAGE_SIZE), lambda b_, j, pt: (b_, 0, 0))],
    )
    return pl.pallas_call(
        functools.partial(_idx_decode_kernel, n_pg=n_pg),
        grid_spec=grid_spec,
        out_shape=[jax.ShapeDtypeStruct((b, 1, n_pages * PAGE_SIZE), F32),
                   jax.ShapeDtypeStruct((b, 1, PAGE_SIZE), F32)],
        compiler_params=_params("parallel", "arbitrary"),
        name="idx_decode",
    )(page_table, iq, iw, new_page, *([pool] * n_pg))


def _select_decode_kernel(score_ref, sel_ref, *, n_valid, topk):
    score = score_ref[...]
    kpos = lax.broadcasted_iota(jnp.int32, (1, score.shape[1]), 1)
    key = _sort_key(jnp.where(kpos < n_valid, score, -jnp.inf))

    def write(c, lo, hi, value):
        sel_ref[:, lo:hi] = value

    _select_topk([key], topk, write, LANES)


def select_decode(score, n_valid, topk):
    b, n = score.shape
    tb = _pick(b, 32, 8)
    return pl.pallas_call(
        functools.partial(_select_decode_kernel, n_valid=n_valid, topk=topk),
        grid=(b // tb,),
        in_specs=[pl.BlockSpec((tb, n), lambda i: (i, 0))],
        out_specs=pl.BlockSpec((tb, n), lambda i: (i, 0)),
        out_shape=jax.ShapeDtypeStruct((b, n), F32),
        compiler_params=_params("parallel"),
        name="select_decode",
    )(score)


def _online_softmax_step(s, valid, v, m_ref, l_ref, acc_ref, v_feature_major=False):
    m_old = m_ref[...]
    m_new = jnp.maximum(m_old, jnp.max(jnp.where(valid, s, NEG), axis=-1, keepdims=True))
    alpha = jnp.exp(m_old - m_new)
    p = jnp.where(valid, jnp.exp(s - m_new), 0.0)
    pv = _dot_nt(p.astype(BF16), v) if v_feature_major else _dot(p.astype(BF16), v)
    l_ref[...] = alpha * l_ref[...] + jnp.sum(p, axis=-1, keepdims=True)
    acc_ref[...] = alpha * acc_ref[...] + pv
    m_ref[...] = m_new


def _dsa_decode_kernel(pt_ref, q_ref, sel_ref, selnew_ref, new_ref, *rest, n_pg):
    pages = rest[:n_pg]
    o_ref, m_ref, l_ref, acc_ref = rest[n_pg:]
    j = pl.program_id(1)

    @pl.when(j == 0)
    def _():
        m_ref[...] = jnp.full_like(m_ref, NEG)
        l_ref[...] = jnp.zeros_like(l_ref)
        acc_ref[...] = jnp.zeros_like(acc_ref)

    q = q_ref[...]
    scale = HEAD_DIM ** -0.5
    k = _cat_rows(pages, 0, 2)
    v = _cat_rows(pages, 1, 2)
    _online_softmax_step(_dot_nt(q, k) * scale, sel_ref[...] > 0.5, v, m_ref, l_ref, acc_ref)

    @pl.when(j == pl.num_programs(1) - 1)
    def _():
        kn = _cat_rows([new_ref], 0, 2)
        vn = _cat_rows([new_ref], 1, 2)
        _online_softmax_step(_dot_nt(q, kn) * scale, selnew_ref[...] > 0.5, vn, m_ref, l_ref, acc_ref)
        o_ref[...] = (acc_ref[...] / l_ref[...]).astype(o_ref.dtype)


def dsa_decode(page_table, q, sel, new_page, pool, layer):
    b, n_pages = page_table.shape
    h = q.shape[1]
    n_pg = _pick(n_pages, 16, 1)
    n_steps = n_pages // n_pg
    grid_spec = pltpu.PrefetchScalarGridSpec(
        num_scalar_prefetch=1,
        grid=(b, n_steps),
        in_specs=[pl.BlockSpec((None, h, HEAD_DIM), lambda b_, j, pt: (b_, 0, 0)),
                  pl.BlockSpec((None, 1, n_pg * PAGE_SIZE), lambda b_, j, pt: (b_, 0, j)),
                  pl.BlockSpec((None, 1, PAGE_SIZE), lambda b_, j, pt: (b_, 0, n_pages)),
                  pl.BlockSpec((None, 2 * PAGE_SIZE, HEAD_DIM), lambda b_, j, pt: (b_, 0, 0))]
        + _page_specs(2 * PAGE_SIZE, HEAD_DIM, layer, n_pg, n_steps, False),
        out_specs=pl.BlockSpec((None, h, HEAD_DIM), lambda b_, j, pt: (b_, 0, 0)),
        scratch_shapes=[pltpu.VMEM((h, 1), F32), pltpu.VMEM((h, 1), F32), pltpu.VMEM((h, HEAD_DIM), F32)],
    )
    return pl.pallas_call(
        functools.partial(_dsa_decode_kernel, n_pg=n_pg),
        grid_spec=grid_spec,
        out_shape=jax.ShapeDtypeStruct((b, h, HEAD_DIM), BF16),
        compiler_params=_params("parallel", "arbitrary"),
        name="dsa_decode",
    )(page_table, q, sel, sel, new_page, *([pool] * n_pg))


def _mla_decode_kernel(pt_ref, q_ref, new_ref, *rest, n_pg):
    pages = rest[:n_pg]
    o_ref, m_ref, l_ref, acc_ref = rest[n_pg:]
    j = pl.program_id(1)

    @pl.when(j == 0)
    def _():
        m_ref[...] = jnp.full_like(m_ref, NEG)
        l_ref[...] = jnp.zeros_like(l_ref)
        acc_ref[...] = jnp.zeros_like(acc_ref)

    qn = q_ref[:, 0:KV_RANK]
    qr = q_ref[:, KV_RANK:KV_RANK + D_ROPE_B]
    scale = (D_NOPE + D_ROPE_B) ** -0.5
    c_t = _cat_lanes(pages, 0, KV_RANK)
    kr_t = _cat_lanes(pages, KV_RANK, KV_RANK + D_ROPE_B)
    s = (_dot(qn, c_t) + _dot(qr, kr_t)) * scale
    _online_softmax_step(s, jnp.full(s.shape, True), c_t, m_ref, l_ref, acc_ref, v_feature_major=True)

    @pl.when(j == pl.num_programs(1) - 1)
    def _():
        cn_t = new_ref[0:KV_RANK, :].astype(BF16)
        krn_t = new_ref[KV_RANK:KV_RANK + D_ROPE_B, :].astype(BF16)
        sn = (_dot(qn, cn_t) + _dot(qr, krn_t)) * scale
        first = lax.broadcasted_iota(jnp.int32, sn.shape, 1) < 1
        _online_softmax_step(sn, first, cn_t, m_ref, l_ref, acc_ref, v_feature_major=True)
        o_ref[...] = (acc_ref[...] / l_ref[...]).astype(o_ref.dtype)


def mla_decode(page_table, q, new_page, pool, layer):
    b, n_pages = page_table.shape
    h = q.shape[1]
    width = KV_RANK + D_ROPE_B
    n_pg = _pick(n_pages, 16, 1)
    n_steps = n_pages // n_pg
    grid_spec = pltpu.PrefetchScalarGridSpec(
        num_scalar_prefetch=1,
        grid=(b, n_steps),
        in_specs=[pl.BlockSpec((None, h, q.shape[2]), lambda b_, j, pt: (b_, 0, 0)),
                  pl.BlockSpec((None, width, PAGE_SIZE), lambda b_, j, pt: (b_, 0, 0))]
        + _page_specs(width, PAGE_SIZE, layer, n_pg, n_steps, False),
        out_specs=pl.BlockSpec((None, h, KV_RANK), lambda b_, j, pt: (b_, 0, 0)),
        scratch_shapes=[pltpu.VMEM((h, 1), F32), pltpu.VMEM((h, 1), F32), pltpu.VMEM((h, KV_RANK), F32)],
    )
    return pl.pallas_call(
        functools.partial(_mla_decode_kernel, n_pg=n_pg),
        grid_spec=grid_spec,
        out_shape=jax.ShapeDtypeStruct((b, h, KV_RANK), BF16),
        compiler_params=_params("parallel", "arbitrary"),
        name="mla_decode",
    )(page_table, q, new_page, *([pool] * n_pg))


def _sb_decode_kernel(pt_ref, q_ref, *rest, n_pg, blk):
    pages = rest[:n_pg]
    o_ref, carry_ref, acc_ref = rest[n_pg:]
    j = pl.program_id(1)

    @pl.when(j == 0)
    def _():
        carry_ref[...] = jnp.zeros_like(carry_ref)
        acc_ref[...] = jnp.zeros_like(acc_ref)

    later = _later_matrix(blk)
    heads = q_ref.shape[1]
    n_blocks = n_pg * PAGE_SIZE // blk
    z_blocks, values = [], []
    for g in range(KVH_C):
        z = _dot_nt(q_ref[g], _cat_rows(pages, g, 2 * KVH_C)) * HEAD_DIM ** -0.5
        z_blocks += [z[:, blk * n:blk * (n + 1)] for n in range(n_blocks)]
        values.append(_cat_rows(pages, KVH_C + g, 2 * KVH_C))
    z = jnp.concatenate(z_blocks, axis=0)
    rows = z.shape[0]
    lk = -_softplus(z)
    hi = lk.astype(BF16)
    lo = (lk - hi.astype(F32)).astype(BF16)
    within = _dot(jnp.concatenate([hi, lo], axis=0), later)
    total = jnp.sum(lk, axis=-1, keepdims=True)
    carries = [None] * (KVH_C * n_blocks)
    for g in range(KVH_C):
        running = carry_ref[g]
        for n in reversed(range(n_blocks)):
            b = g * n_blocks + n
            carries[b] = running
            running = running + total[heads * b:heads * (b + 1)]
        carry_ref[g] = running
    tail = within[:rows] + within[rows:] + jnp.concatenate(carries, axis=0)
    a = jnp.exp(z + lk + tail).astype(BF16)
    for g in range(KVH_C):
        a_g = jnp.concatenate([a[heads * (g * n_blocks + n):heads * (g * n_blocks + n + 1)]
                               for n in range(n_blocks)], axis=1)
        acc_ref[g] = acc_ref[g] + _dot(a_g, values[g])

    @pl.when(j == pl.num_programs(1) - 1)
    def _():
        o_ref[...] = acc_ref[...].astype(o_ref.dtype)


def sb_decode(page_table, q, pool, layer):
    b, n_pages = page_table.shape
    heads = q.shape[2]
    n_pg = _pick(n_pages, 16, 1)
    n_steps = n_pages // n_pg
    blk = _pick(n_pg * PAGE_SIZE, 256, LANES)
    grid_spec = pltpu.PrefetchScalarGridSpec(
        num_scalar_prefetch=1,
        grid=(b, n_steps),
        in_specs=[pl.BlockSpec((None, KVH_C, heads, HEAD_DIM), lambda b_, j, pt: (b_, 0, 0, 0))]
        + _page_specs(2 * KVH_C * PAGE_SIZE, HEAD_DIM, layer, n_pg, n_steps, True),
        out_specs=pl.BlockSpec((None, KVH_C, heads, HEAD_DIM), lambda b_, j, pt: (b_, 0, 0, 0)),
        scratch_shapes=[pltpu.VMEM((KVH_C, heads, 1), F32), pltpu.VMEM((KVH_C, heads, HEAD_DIM), F32)],
    )
    return pl.pallas_call(
        functools.partial(_sb_decode_kernel, n_pg=n_pg, blk=blk),
        grid_spec=grid_spec,
        out_shape=jax.ShapeDtypeStruct((b, KVH_C, heads, HEAD_DIM), BF16),
        compiler_params=_params("parallel", "arbitrary"),
        name="sb_decode",
    )(page_table, q, *([pool] * n_pg))


def _pad_cols(w, width):
    return jnp.pad(w, ((0, 0), (0, width - w.shape[1])))


def _pad_row(g, width):
    return jnp.pad(g, (0, width - g.shape[0])).reshape(1, width)


def _new_page_rows(rows, rows_per_pos):
    b = rows.shape[0]
    r = rows.reshape(b, rows_per_pos, -1)
    return jnp.pad(r, ((0, 0), (0, rows_per_pos * (PAGE_SIZE - 1)), (0, 0)))


def _new_page_feature_major(rows):
    return jnp.pad(rows[:, :, None], ((0, 0), (0, 0), (0, PAGE_SIZE - 1)))


def even_layer(x, tabs, n_prompt, batch, t, page_table, layer, cache_a_kv, cache_a_idx, cache_b_latent,
               e_norm, e_w_in, a_qnorm, a_knorm, idx_knorm, b_qa_norm, b_wq_b, b_kv_norm, b_wkv_b,
               b_qnorm, b_krnorm, e_w_out):
    m, d = x.shape
    h_a = d // 2 // HEAD_DIM
    h_b = h_a
    past_len = page_table.shape[1] * PAGE_SIZE
    dec_b = m - n_prompt

    widths = (h_a * HEAD_DIM, HEAD_DIM, HEAD_DIM, H_IDX * D_IDX, H_IDX, D_IDX, Q_RANK, KV_RANK, D_ROPE_B)
    src = np.cumsum((0,) + (h_a * HEAD_DIM, HEAD_DIM, HEAD_DIM, H_IDX * D_IDX, H_IDX, D_IDX, Q_RANK, KV_RANK))
    padded = tuple(-(-w // LANES) * LANES for w in widths)
    offs = tuple(int(o) for o in np.cumsum((0,) + padded)[:-1])
    total = -(-sum(padded) // 512) * 512
    cols = [_pad_cols(e_w_in[:, int(s):int(s) + w], p) for s, w, p in zip(src, widths, padded)]
    w_in = _pad_cols(jnp.concatenate(cols, axis=1), total).astype(BF16)

    h = rms_norm_rows(x, e_norm)
    proj = matmul(h, w_in)
    gains = (a_qnorm.reshape(1, -1), a_knorm.reshape(1, -1), _pad_row(idx_knorm, LANES),
             b_qa_norm.reshape(1, -1), b_kv_norm.reshape(1, -1), _pad_row(b_krnorm, LANES))
    qa, kv, iq, iw, ik, ika, ikb, qlo, lat = even_post(proj, tabs, gains, h_a, offs)

    wq = b_wq_b.reshape(Q_RANK, h_b, D_NOPE + D_ROPE_B)
    wq_nope = wq[:, :, :D_NOPE].reshape(Q_RANK, h_b * D_NOPE)
    wq_rope = jnp.pad(wq[:, :, D_NOPE:], ((0, 0), (0, 0), (0, LANES - D_ROPE_B))).reshape(Q_RANK, h_b * LANES)
    qb = matmul(qlo, jnp.concatenate([wq_nope, wq_rope], axis=1).astype(BF16))
    wuk = jnp.transpose(b_wkv_b[:, :, :D_NOPE], (1, 2, 0)).astype(BF16)
    wuv = jnp.transpose(b_wkv_b[:, :, D_NOPE:], (1, 0, 2)).astype(BF16)
    qlat = mla_q(qb, tabs, b_qnorm[:D_NOPE].reshape(1, -1), _pad_row(b_qnorm[D_NOPE:], LANES), wuk, h_b)

    topk_p = min(TOPK_MAX, t // 4)
    oa_p = dsa_prompt(iq, iw, ika, ikb, qa, kv, batch, t, h_a, topk_p)
    ob_p = mla_prompt(qlat, lat, wuv, batch, t, h_b)

    topk_s = min(TOPK_MAX, (past_len + 1) // 4)
    pool_kv = cache_a_kv.reshape(cache_a_kv.shape[:2] + (2 * PAGE_SIZE, HEAD_DIM))
    pool_idx = jnp.swapaxes(cache_a_idx, 2, 3)
    pool_lat = jnp.swapaxes(cache_b_latent, 2, 3)
    iq_s = iq[n_prompt:].reshape(dec_b, H_IDX, D_IDX)
    iw_s = iw[n_prompt:, :H_IDX].reshape(dec_b, H_IDX, 1)
    sc_old, sc_new = idx_decode(page_table, iq_s, iw_s, _new_page_feature_major(ik[n_prompt:]), pool_idx, layer)
    sel = select_decode(jnp.concatenate([sc_old[:, 0], sc_new[:, 0]], axis=1), past_len + 1, topk_s)
    oa_s = dsa_decode(page_table, qa[n_prompt:].reshape(dec_b, h_a, HEAD_DIM), sel[:, None, :],
                      _new_page_rows(kv[n_prompt:], 2), pool_kv, layer)
    ctx_s = mla_decode(page_table, qlat[n_prompt:].reshape(dec_b, h_b, KV_RANK + LANES),
                       _new_page_feature_major(lat[n_prompt:]), pool_lat, layer)
    ob_s = head_out(ctx_s.reshape(dec_b, h_b * KV_RANK), wuv)

    mix_in = jnp.concatenate([jnp.concatenate([oa_p, ob_p], axis=1),
                              jnp.concatenate([oa_s.reshape(dec_b, -1), ob_s], axis=1)], axis=0)
    x = matmul_residual(mix_in, e_w_out.astype(BF16), x, 1.0)
    return x, kv, ik, lat


def odd_layer(x, n_prompt, batch, t, page_table, layer, cache_c_kv, o_norm, o_w_in, o_w_out):
    m, d = x.shape
    h_c = d // HEAD_DIM
    dec_b = m - n_prompt
    h = rms_norm_rows(x, o_norm)
    proj = matmul(h, o_w_in.astype(BF16))
    o_p = sb_prompt(proj, batch, t, h_c)
    q_s = proj[n_prompt:, :h_c * HEAD_DIM].astype(BF16).reshape(dec_b, KVH_C, h_c // KVH_C, HEAD_DIM)
    pool = cache_c_kv.reshape(cache_c_kv.shape[:2] + (2 * KVH_C * PAGE_SIZE, HEAD_DIM))
    o_s = sb_decode(page_table, q_s, pool, layer)
    mix_in = jnp.concatenate([o_p, o_s.reshape(dec_b, -1)], axis=0)
    x = matmul_residual(mix_in, o_w_out.astype(BF16), x, 1.0)
    return x, proj[:, h_c * HEAD_DIM:]


def kernel(x_prompt, x_sample, cache_a_kv, cache_a_idx, cache_b_latent, cache_c_kv, page_table, ffn1_norm, ffn1_w_in, ffn1_w_out, ffn2_norm, ffn2_w_in, ffn2_w_out, e_norm, e_w_in, e_a_qnorm, e_a_knorm, e_idx_knorm, e_b_qa_norm, e_b_wq_b, e_b_kv_norm, e_b_wkv_b, e_b_qnorm, e_b_krnorm, e_w_out, o_norm, o_w_in, o_w_out):
    batch, t, d = x_prompt.shape
    dec_b, dec_t, _ = x_sample.shape
    assert dec_t == 1, "one new token per sampled sequence"
    depth = ffn1_norm.shape[0]
    n_prompt = batch * t
    past_len = page_table.shape[1] * PAGE_SIZE
    x = jnp.concatenate([x_prompt.reshape(n_prompt, d), x_sample.reshape(dec_b, d)], axis=0)

    pos = jnp.concatenate([jnp.tile(jnp.arange(t, dtype=F32), batch),
                           jnp.full((dec_b,), past_len, F32)]).reshape(-1, 1)
    rows = jnp.stack([_rope_rows(HEAD_DIM // 4, HEAD_DIM), _rope_rows(D_IDX // 4, D_IDX),
                      _rope_rows(D_ROPE_B, LANES)])
    tabs = rope_tables(pos, rows)

    akv, aidx, blat, ckv = [], [], [], []
    for layer in range(depth):
        x = ffn_half_step(x, ffn1_norm[layer], ffn1_w_in[layer].astype(BF16), ffn1_w_out[layer].astype(BF16))
        if layer % 2 == 0:
            e = layer // 2
            x, kv, ik, lat = even_layer(
                x, tabs, n_prompt, batch, t, page_table, e, cache_a_kv, cache_a_idx, cache_b_latent,
                e_norm[e], e_w_in[e], e_a_qnorm[e], e_a_knorm[e], e_idx_knorm[e], e_b_qa_norm[e], e_b_wq_b[e],
                e_b_kv_norm[e], e_b_wkv_b[e], e_b_qnorm[e], e_b_krnorm[e], e_w_out[e])
            akv.append(kv)
            aidx.append(ik)
            blat.append(lat)
        else:
            o = layer // 2
            x, kvc = odd_layer(x, n_prompt, batch, t, page_table, o, cache_c_kv, o_norm[o], o_w_in[o], o_w_out[o])
            ckv.append(kvc)
        x = ffn_half_step(x, ffn2_norm[layer], ffn2_w_in[layer].astype(BF16), ffn2_w_out[layer].astype(BF16),
                          split_at=n_prompt if layer == depth - 1 else None)
    y_prompt, y_sample = x

    def split(rows_list, tail):
        a = jnp.stack(rows_list)
        return (a[:, :n_prompt].reshape((len(rows_list), batch, t) + tail),
                a[:, n_prompt:].reshape((len(rows_list), dec_b, dec_t) + tail))

    akv_p, akv_s = split(akv, (2, 1, HEAD_DIM))
    aidx_p, aidx_s = split(aidx, (D_IDX,))
    blat_p, blat_s = split(blat, (KV_RANK + D_ROPE_B,))
    ckv_p, ckv_s = split(ckv, (2, KVH_C, HEAD_DIM))
    return (y_prompt.reshape(batch, t, d), y_sample.reshape(dec_b, dec_t, d),
            akv_p, akv_s, aidx_p, aidx_s, blat_p, blat_s, ckv_p, ckv_s)
```

```python
import functools

import numpy as np
import jax
import jax.numpy as jnp
from jax import lax
from jax.experimental import pallas as pl
from jax.experimental.pallas import tpu as pltpu

HEAD_DIM = 128
ROPE_THETA = 500000.0
EPS = 1e-6
PAGE_SIZE = 128
H_IDX = 32
D_IDX = 64
TOPK_MAX = 256
Q_RANK = 768
KV_RANK = 256
D_NOPE = 128
D_ROPE_B = 64
D_VB = 128
KVH_C = 2
LANES = 128
V7X_VMEM_LIMIT = 56 * 1024 * 1024
NEG = -1e30
INT_MIN = -(2 ** 31)
NEG_INF_KEY = INT_MIN + 0x7FFFFF

BF16 = jnp.bfloat16
F32 = jnp.float32


def _pick(n, target, mult):
    best = None
    for d in range(mult, min(n, target) + 1, mult):
        if n % d == 0:
            best = d
    assert best is not None, (n, target, mult)
    return best


def _params(*sem):
    return pltpu.CompilerParams(dimension_semantics=sem, vmem_limit_bytes=V7X_VMEM_LIMIT)


def _dot(a, b):
    return jnp.dot(a, b, preferred_element_type=F32)


def _dot_nt(a, b):
    return lax.dot_general(a, b, (((1,), (1,)), ((), ())), preferred_element_type=F32)


def _norm_kernel(x_ref, g_ref, o_ref):
    x = x_ref[...]
    y = x * lax.rsqrt(jnp.mean(x * x, axis=-1, keepdims=True) + EPS)
    o_ref[...] = (y * g_ref[...]).astype(o_ref.dtype)


def rms_norm_rows(x, g):
    m, d = x.shape
    tm = _pick(m, 512, 16)
    return pl.pallas_call(
        _norm_kernel,
        grid=(m // tm,),
        in_specs=[pl.BlockSpec((tm, d), lambda i: (i, 0)), pl.BlockSpec((1, d), lambda i: (0, 0))],
        out_specs=pl.BlockSpec((tm, d), lambda i: (i, 0)),
        out_shape=jax.ShapeDtypeStruct((m, d), BF16),
        compiler_params=_params("parallel"),
        name="rms_norm_rows",
    )(x, g.reshape(1, d))


def _mm_kernel(a_ref, w_ref, o_ref):
    o_ref[...] = _dot(a_ref[...], w_ref[...]).astype(o_ref.dtype)


def matmul(a, w, out_dtype=F32, tm_target=1664, tn_target=512):
    m, k = a.shape
    n = w.shape[1]
    tm = _pick(m, tm_target, 16)
    tn = _pick(n, tn_target, LANES)
    return pl.pallas_call(
        _mm_kernel,
        grid=(m // tm, n // tn),
        in_specs=[pl.BlockSpec((tm, k), lambda i, j: (i, 0)), pl.BlockSpec((k, tn), lambda i, j: (0, j))],
        out_specs=pl.BlockSpec((tm, tn), lambda i, j: (i, j)),
        out_shape=jax.ShapeDtypeStruct((m, n), out_dtype),
        compiler_params=_params("parallel", "parallel"),
        name="matmul",
    )(a, w)


def _mm_res_kernel(a_ref, w_ref, r_ref, o_ref, *, scale):
    o_ref[...] = r_ref[...] + scale * _dot(a_ref[...], w_ref[...])


def _weight_spec(w, layer, tn, col_block0=0):
    if w.ndim == 2:
        return pl.BlockSpec((w.shape[0], tn), lambda i, j: (0, j + col_block0))
    return pl.BlockSpec((None, w.shape[1], tn), lambda i, j: (layer, 0, j + col_block0))


def matmul_residual(a, w, res, scale, layer=None, tm_target=416, tn_target=512, row_start=0, n_rows=None):
    k = a.shape[1]
    n = w.shape[-1]
    n_rows = a.shape[0] if n_rows is None else n_rows
    tm = _pick(int(np.gcd(row_start, n_rows)), tm_target, 16)
    tn = _pick(n, tn_target, LANES)
    i0 = row_start // tm
    return pl.pallas_call(
        functools.partial(_mm_res_kernel, scale=scale),
        grid=(n_rows // tm, n // tn),
        in_specs=[pl.BlockSpec((tm, k), lambda i, j: (i + i0, 0)), _weight_spec(w, layer, tn),
                  pl.BlockSpec((tm, tn), lambda i, j: (i + i0, j))],
        out_specs=pl.BlockSpec((tm, tn), lambda i, j: (i, j)),
        out_shape=jax.ShapeDtypeStruct((n_rows, n), F32),
        compiler_params=_params("parallel", "parallel"),
        name="matmul_residual",
    )(a, w, res)


def _ffn_in_kernel(a_ref, wg_ref, wu_ref, o_ref):
    a = a_ref[...]
    g = _dot(a, wg_ref[...])
    u = _dot(a, wu_ref[...])
    o_ref[...] = (g * jax.nn.sigmoid(g) * u).astype(o_ref.dtype)


def ffn_in(h, w_in, layer, tm_target=1664, tn_target=256):
    m, k = h.shape
    f = w_in.shape[-1] // 2
    tm = _pick(m, tm_target, 16)
    tn = _pick(f, tn_target, LANES)
    nj = f // tn
    return pl.pallas_call(
        _ffn_in_kernel,
        grid=(m // tm, nj),
        in_specs=[pl.BlockSpec((tm, k), lambda i, j: (i, 0)), _weight_spec(w_in, layer, tn),
                  _weight_spec(w_in, layer, tn, nj)],
        out_specs=pl.BlockSpec((tm, tn), lambda i, j: (i, j)),
        out_shape=jax.ShapeDtypeStruct((m, f), BF16),
        compiler_params=_params("parallel", "parallel"),
        name="ffn_in",
    )(h, w_in, w_in)


def ffn_half_step(x, g, w_in, w_out, layer, split_at=None):
    h = rms_norm_rows(x, g)
    act = ffn_in(h, w_in, layer)
    if split_at is None:
        return matmul_residual(act, w_out, x, 0.5, layer)
    head = matmul_residual(act, w_out, x, 0.5, layer, tm_target=512, tn_target=256, n_rows=split_at)
    tail = matmul_residual(act, w_out, x, 0.5, layer, row_start=split_at, n_rows=x.shape[0] - split_at)
    return head, tail


def _rope_rows(rot, width):
    half = rot // 2
    j = np.arange(LANES) % width
    in_rot = (j < rot).astype(np.float32)
    take_lo = ((j >= half) & (j < rot)).astype(np.float32)
    take_hi = (j < half).astype(np.float32)
    inv = ROPE_THETA ** (-jnp.arange(half, dtype=F32) / half)
    inv_row = jnp.where(jnp.asarray(j < rot), inv[j % half], 0.0).astype(F32)
    return jnp.stack([inv_row, jnp.asarray(in_rot), jnp.asarray(take_lo), jnp.asarray(take_hi)])


def _rope_table_kernel(pos_ref, rows_ref, o_ref):
    pos = pos_ref[...]
    for f in range(rows_ref.shape[0]):
        rows = rows_ref[f]
        ang = pos * rows[0:1]
        cos = jnp.cos(ang)
        sin = jnp.sin(ang)
        o_ref[f, 0] = jnp.where(rows[1:2] > 0.5, cos, 1.0)
        o_ref[f, 1] = rows[2:3] * sin
        o_ref[f, 2] = -rows[3:4] * sin


def rope_tables(pos, rows):
    m = pos.shape[0]
    nf = rows.shape[0]
    tm = _pick(m, 512, 8)
    return pl.pallas_call(
        _rope_table_kernel,
        grid=(m // tm,),
        in_specs=[pl.BlockSpec((tm, 1), lambda i: (i, 0)), pl.BlockSpec((nf, 4, LANES), lambda i: (0, 0, 0))],
        out_specs=pl.BlockSpec((nf, 3, tm, LANES), lambda i: (0, 0, i, 0)),
        out_shape=jax.ShapeDtypeStruct((nf, 3, m, LANES), F32),
        compiler_params=_params("parallel"),
        name="rope_tables",
    )(pos, rows)


def _rope(x, tab, half):
    return x * tab[0] + pltpu.roll(x, half, 1) * tab[1] + pltpu.roll(x, LANES - half, 1) * tab[2]


def _head_norm(x, g, n):
    return x * lax.rsqrt(jnp.sum(x * x, axis=-1, keepdims=True) / n + EPS) * g


def _even_post_kernel(p_ref, tab_ref, gq_ref, gk_ref, gik_ref, gqa_ref, gkv_ref, gkr_ref,
                      qa_o, kv_o, iq_o, iw_o, ik_o, ika_o, ikb_o, qlo_o, lat_o, *, h_a, offs):
    o_qa, o_ka, o_va, o_iq, o_iw, o_ik, o_qlo, o_c, o_kr = offs
    tab_a = tab_ref[0]
    tab_i = tab_ref[1]
    tab_b = tab_ref[2]
    rot_a = HEAD_DIM // 4
    rot_i = D_IDX // 4
    for h in range(h_a):
        x = p_ref[:, o_qa + LANES * h:o_qa + LANES * (h + 1)]
        y = _rope(_head_norm(x, gq_ref[...], HEAD_DIM), tab_a, rot_a // 2)
        qa_o[:, LANES * h:LANES * (h + 1)] = y.astype(qa_o.dtype)
    ka = _rope(_head_norm(p_ref[:, o_ka:o_ka + LANES], gk_ref[...], HEAD_DIM), tab_a, rot_a // 2)
    kv_o[:, 0:LANES] = ka
    kv_o[:, LANES:2 * LANES] = p_ref[:, o_va:o_va + LANES]
    for b in range(H_IDX * D_IDX // LANES):
        x = p_ref[:, o_iq + LANES * b:o_iq + LANES * (b + 1)]
        iq_o[:, LANES * b:LANES * (b + 1)] = _rope(x, tab_i, rot_i // 2).astype(iq_o.dtype)
    iw_o[...] = p_ref[:, o_iw:o_iw + LANES] * (H_IDX * D_IDX) ** -0.5
    ik = _rope(_head_norm(p_ref[:, o_ik:o_ik + LANES], gik_ref[...], D_IDX), tab_i, rot_i // 2)
    ik_o[...] = ik[:, :D_IDX]
    ika_o[...] = ik.astype(ika_o.dtype)
    ikb_o[...] = pltpu.roll(ik, D_IDX, 1).astype(ikb_o.dtype)
    qlo = p_ref[:, o_qlo:o_qlo + Q_RANK]
    qlo_o[...] = _head_norm(qlo, gqa_ref[...], Q_RANK).astype(qlo_o.dtype)
    lat_o[:, 0:KV_RANK] = _head_norm(p_ref[:, o_c:o_c + KV_RANK], gkv_ref[...], KV_RANK)
    kr = _rope(_head_norm(p_ref[:, o_kr:o_kr + LANES], gkr_ref[...], D_ROPE_B), tab_b, D_ROPE_B // 2)
    lat_o[:, KV_RANK:KV_RANK + D_ROPE_B] = kr[:, :D_ROPE_B]


def even_post(proj, tabs, gains, h_a, offs):
    m, npj = proj.shape
    tm = _pick(m, 320, 16)
    row = lambda w: pl.BlockSpec((tm, w), lambda i: (i, 0))
    full = lambda a: pl.BlockSpec(a.shape, lambda i: (0,) * a.ndim)
    out_shape = [
        jax.ShapeDtypeStruct((m, h_a * HEAD_DIM), BF16),
        jax.ShapeDtypeStruct((m, 2 * HEAD_DIM), F32),
        jax.ShapeDtypeStruct((m, H_IDX * D_IDX), BF16),
        jax.ShapeDtypeStruct((m, LANES), F32),
        jax.ShapeDtypeStruct((m, D_IDX), F32),
        jax.ShapeDtypeStruct((m, LANES), BF16),
        jax.ShapeDtypeStruct((m, LANES), BF16),
        jax.ShapeDtypeStruct((m, Q_RANK), BF16),
        jax.ShapeDtypeStruct((m, KV_RANK + D_ROPE_B), F32),
    ]
    return pl.pallas_call(
        functools.partial(_even_post_kernel, h_a=h_a, offs=offs),
        grid=(m // tm,),
        in_specs=[row(npj), pl.BlockSpec((3, 3, tm, LANES), lambda i: (0, 0, i, 0))] + [full(g) for g in gains],
        out_specs=[row(s.shape[1]) for s in out_shape],
        out_shape=out_shape,
        compiler_params=_params("parallel"),
        name="even_post",
    )(proj, tabs, *gains)


def _mla_q_kernel(qb_ref, tab_ref, gn_ref, gr_ref, wuk_ref, o_ref, *, h_b):
    tab_b = tab_ref[0]
    qw = KV_RANK + LANES
    for h in range(h_b):
        xn = qb_ref[:, LANES * h:LANES * (h + 1)]
        xr = qb_ref[:, LANES * (h_b + h):LANES * (h_b + h + 1)]
        ss = jnp.sum(xn * xn, axis=-1, keepdims=True) + jnp.sum(xr * xr, axis=-1, keepdims=True)
        r = lax.rsqrt(ss / (D_NOPE + D_ROPE_B) + EPS)
        yn = xn * r * gn_ref[...]
        yr = _rope(xr * r * gr_ref[...], tab_b, D_ROPE_B // 2)
        o_ref[:, qw * h:qw * h + KV_RANK] = _dot(yn.astype(BF16), wuk_ref[h]).astype(o_ref.dtype)
        o_ref[:, qw * h + KV_RANK:qw * (h + 1)] = yr.astype(o_ref.dtype)


def mla_q(qb, tabs, gn, gr, wuk, h_b):
    m = qb.shape[0]
    tm = _pick(m, 320, 16)
    qw = KV_RANK + LANES
    return pl.pallas_call(
        functools.partial(_mla_q_kernel, h_b=h_b),
        grid=(m // tm,),
        in_specs=[pl.BlockSpec((tm, qb.shape[1]), lambda i: (i, 0)),
                  pl.BlockSpec((1, 3, tm, LANES), lambda i: (2, 0, i, 0)),
                  pl.BlockSpec((1, LANES), lambda i: (0, 0)), pl.BlockSpec((1, LANES), lambda i: (0, 0)),
                  pl.BlockSpec(wuk.shape, lambda i: (0, 0, 0))],
        out_specs=pl.BlockSpec((tm, h_b * qw), lambda i: (i, 0)),
        out_shape=jax.ShapeDtypeStruct((m, h_b * qw), BF16),
        compiler_params=_params("parallel"),
        name="mla_q",
    )(qb, tabs, gn, gr, wuk)


def _head_out_kernel(c_ref, w_ref, o_ref, *, n_heads):
    kw = w_ref.shape[1]
    nw = w_ref.shape[2]
    for h in range(n_heads):
        o_ref[:, nw * h:nw * (h + 1)] = _dot(c_ref[:, kw * h:kw * (h + 1)], w_ref[h]).astype(o_ref.dtype)


def head_out(c, w):
    m = c.shape[0]
    n_heads, kw, nw = w.shape
    return pl.pallas_call(
        functools.partial(_head_out_kernel, n_heads=n_heads),
        grid=(1,),
        in_specs=[pl.BlockSpec(c.shape, lambda i: (0, 0)), pl.BlockSpec(w.shape, lambda i: (0, 0, 0))],
        out_specs=pl.BlockSpec((m, n_heads * nw), lambda i: (0, 0)),
        out_shape=jax.ShapeDtypeStruct((m, n_heads * nw), BF16),
        compiler_params=_params("arbitrary"),
        name="head_out",
    )(c, w)


def _sort_key(score):
    bits = pltpu.bitcast(score + 0.0, jnp.int32)
    return bits ^ ((bits >> 31) & 0x7FFFFFFF)


def _count(keys, pred):
    return sum(jnp.sum(jnp.where(pred(key), 1.0, 0.0), axis=-1, keepdims=True) for key in keys)


def _kth_largest_key(keys, k):
    rows = keys[0].shape[0]

    def body(i, t):
        cand = t | jnp.left_shift(jnp.int32(1), 31 - i)
        cnt = _count(keys, lambda key: key >= (cand ^ INT_MIN))
        return jnp.where(cnt >= k, cand, t)

    t = lax.fori_loop(0, 32, body, jnp.zeros((rows, 1), jnp.int32))
    return t ^ INT_MIN


def _select_topk(keys, k, write, blk):
    rows = keys[0].shape[0]
    thr = _kth_largest_key(keys, k)
    need = k - _count(keys, lambda key: key > thr)
    ri = lax.broadcasted_iota(jnp.int32, (blk, blk), 0)
    ci = lax.broadcasted_iota(jnp.int32, (blk, blk), 1)
    earlier = jnp.where(ri < ci, 1.0, 0.0).astype(BF16)
    seen = jnp.zeros((rows, 1), F32)
    for c, key in enumerate(keys):
        for b in range(key.shape[1] // blk):
            kb = key[:, blk * b:blk * (b + 1)]
            e = jnp.where(kb == thr, 1.0, 0.0)
            rank = _dot(e.astype(BF16), earlier) + seen
            seen = seen + jnp.sum(e, axis=-1, keepdims=True)
            take = jnp.where(rank < need, e, 0.0)
            write(c, blk * b, blk * (b + 1), jnp.where(kb > thr, 1.0, take))


def _stack_heads(src_ref, dst_ref, heads, tq, width):
    for h in range(heads):
        dst_ref[tq * h:tq * (h + 1), :] = src_ref[:, width * h:width * (h + 1)].astype(dst_ref.dtype)


def _tile_rows(x, reps):
    return jnp.concatenate([x] * reps, axis=0)


def _dsa_prompt_kernel(iq_ref, iw_ref, ika_ref, ikb_ref, qa_ref, kv_ref, o_ref,
                       iqs_ref, qas_ref, key_ref, sel_ref, m_ref, l_ref, acc_ref, *, tq, h_a, topk, ck):
    i = pl.program_id(1)
    n_chunks = kv_ref.shape[0] // ck
    n_pairs = H_IDX // 2
    _stack_heads(iq_ref, iqs_ref, n_pairs, tq, LANES)
    _stack_heads(qa_ref, qas_ref, h_a, tq, HEAD_DIM)
    qpos = i * tq + lax.broadcasted_iota(jnp.int32, (tq, 1), 0)
    lane = lax.broadcasted_iota(jnp.int32, (1, ck), 1)
    last = i * tq + tq - 1
    iw = iw_ref[...]

    for c in range(n_chunks):
        @pl.when(c * ck <= last)
        def _():
            sa = jnp.maximum(_dot_nt(iqs_ref[...], ika_ref[ck * c:ck * (c + 1), :]), 0.0)
            sb = jnp.maximum(_dot_nt(iqs_ref[...], ikb_ref[ck * c:ck * (c + 1), :]), 0.0)
            score = jnp.zeros((tq, ck), F32)
            for pr in range(n_pairs):
                score = score + iw[:, 2 * pr:2 * pr + 1] * sa[tq * pr:tq * (pr + 1)]
                score = score + iw[:, 2 * pr + 1:2 * pr + 2] * sb[tq * pr:tq * (pr + 1)]
            key_ref[c] = _sort_key(jnp.where((ck * c + lane) <= qpos, score, -jnp.inf))

        @pl.when(c * ck > last)
        def _():
            key_ref[c] = jnp.full((tq, ck), NEG_INF_KEY, jnp.int32)

    def write(c, lo, hi, value):
        sel_ref[c, :, lo:hi] = value

    _select_topk([key_ref[c] for c in range(n_chunks)], topk, write, 256)

    _init_softmax_state(m_ref, l_ref, acc_ref)
    for c in range(n_chunks):
        @pl.when(c * ck <= last)
        def _():
            k = kv_ref[ck * c:ck * (c + 1), 0:HEAD_DIM].astype(BF16)
            v = kv_ref[ck * c:ck * (c + 1), HEAD_DIM:2 * HEAD_DIM].astype(BF16)
            s = _dot_nt(qas_ref[...], k) * HEAD_DIM ** -0.5
            chosen = jnp.where((ck * c + lane) <= qpos, sel_ref[c], 0.0)
            state = _online_softmax(s, _tile_rows(chosen, h_a) > 0.5, v, (m_ref[...], l_ref[...], acc_ref[...]))
            m_ref[...], l_ref[...], acc_ref[...] = state

    out = acc_ref[...] / l_ref[...]
    for h in range(h_a):
        o_ref[:, HEAD_DIM * h:HEAD_DIM * (h + 1)] = out[tq * h:tq * (h + 1)].astype(o_ref.dtype)


def dsa_prompt(iq, iw, ika, ikb, qa, kv, batch, t, h_a, topk):
    tq = _pick(t, 128, 16)
    nq = t // tq
    ck = _pick(t, 512, 256)
    qrow = lambda w: pl.BlockSpec((tq, w), lambda b, i: (b * nq + i, 0))
    krow = lambda w: pl.BlockSpec((t, w), lambda b, i: (b, 0))
    return pl.pallas_call(
        functools.partial(_dsa_prompt_kernel, tq=tq, h_a=h_a, topk=topk, ck=ck),
        grid=(batch, nq),
        in_specs=[qrow(H_IDX * D_IDX), qrow(LANES), krow(LANES), krow(LANES), qrow(h_a * HEAD_DIM),
                  krow(2 * HEAD_DIM)],
        out_specs=qrow(h_a * HEAD_DIM),
        out_shape=jax.ShapeDtypeStruct((batch * t, h_a * HEAD_DIM), BF16),
        scratch_shapes=[pltpu.VMEM((H_IDX // 2 * tq, LANES), BF16), pltpu.VMEM((h_a * tq, HEAD_DIM), BF16),
                        pltpu.VMEM((t // ck, tq, ck), jnp.int32), pltpu.VMEM((t // ck, tq, ck), F32),
                        pltpu.VMEM((h_a * tq, 1), F32), pltpu.VMEM((h_a * tq, 1), F32),
                        pltpu.VMEM((h_a * tq, HEAD_DIM), F32)],
        compiler_params=_params("parallel", "parallel"),
        name="dsa_prompt",
    )(iq, iw, ika, ikb, qa, kv)


def _mla_prompt_kernel(q_ref, lat_ref, wuv_ref, o_ref, qs_ref, *, tq, h_b, ck):
    i = pl.program_id(1)
    rows = h_b * tq
    _stack_heads(q_ref, qs_ref, h_b, tq, KV_RANK + LANES)
    qpos = _tile_rows(i * tq + lax.broadcasted_iota(jnp.int32, (tq, 1), 0), h_b)
    lane = lax.broadcasted_iota(jnp.int32, (1, ck), 1)
    scale = (D_NOPE + D_ROPE_B) ** -0.5

    def body(n, state):
        m_old, l_old, acc = state
        start = pl.multiple_of(n * ck, ck)
        c = lat_ref[pl.ds(start, ck), 0:KV_RANK].astype(BF16)
        kr = lat_ref[pl.ds(start, ck), KV_RANK:KV_RANK + D_ROPE_B].astype(BF16)
        s = (_dot_nt(qs_ref[:, 0:KV_RANK], c) + _dot_nt(qs_ref[:, KV_RANK:KV_RANK + D_ROPE_B], kr)) * scale
        valid = (start + lane) <= qpos
        m_new = jnp.maximum(m_old, jnp.max(jnp.where(valid, s, NEG), axis=-1, keepdims=True))
        alpha = jnp.exp(m_old - m_new)
        p = jnp.where(valid, jnp.exp(s - m_new), 0.0)
        l_new = alpha * l_old + jnp.sum(p, axis=-1, keepdims=True)
        return m_new, l_new, alpha * acc + _dot(p.astype(BF16), c)

    n_chunks = (i * tq + tq - 1) // ck + 1
    init = (jnp.full((rows, 1), NEG, F32), jnp.zeros((rows, 1), F32), jnp.zeros((rows, KV_RANK), F32))
    _, l_fin, acc = lax.fori_loop(0, n_chunks, body, init)
    ctx = (acc / l_fin).astype(BF16)
    for h in range(h_b):
        o_ref[:, D_VB * h:D_VB * (h + 1)] = _dot(ctx[tq * h:tq * (h + 1)], wuv_ref[h]).astype(o_ref.dtype)


def mla_prompt(qlat, lat, wuv, batch, t, h_b):
    tq = _pick(t, 128, 16)
    nq = t // tq
    ck = _pick(t, 512, 256)
    return pl.pallas_call(
        functools.partial(_mla_prompt_kernel, tq=tq, h_b=h_b, ck=ck),
        grid=(batch, nq),
        in_specs=[pl.BlockSpec((tq, qlat.shape[1]), lambda b, i: (b * nq + i, 0)),
                  pl.BlockSpec((t, lat.shape[1]), lambda b, i: (b, 0)),
                  pl.BlockSpec(wuv.shape, lambda b, i: (0, 0, 0))],
        out_specs=pl.BlockSpec((tq, h_b * D_VB), lambda b, i: (b * nq + i, 0)),
        out_shape=jax.ShapeDtypeStruct((batch * t, h_b * D_VB), BF16),
        scratch_shapes=[pltpu.VMEM((h_b * tq, KV_RANK + LANES), BF16)],
        compiler_params=_params("parallel", "parallel"),
        name="mla_prompt",
    )(qlat, lat, wuv)


def _later_matrix(blk):
    ri = lax.broadcasted_iota(jnp.int32, (blk, blk), 0)
    ci = lax.broadcasted_iota(jnp.int32, (blk, blk), 1)
    return jnp.where(ri > ci, 1.0, 0.0).astype(BF16)


def _log_keep(neg_z):
    return jnp.minimum(neg_z, 0.0) - jnp.log(1.0 + jnp.exp(-jnp.abs(neg_z)))


def _stick_block(neg_z, before, later, carry):
    lk = _log_keep(neg_z)
    if before is not None:
        lk = jnp.where(before, lk, 0.0)
    hi = lk.astype(BF16)
    lo = (lk - hi.astype(F32)).astype(BF16)
    tail = _dot(hi, later) + _dot(lo, later) + carry
    a = jnp.exp(lk - neg_z + tail)
    if before is not None:
        a = jnp.where(before, a, 0.0)
    return a, carry + jnp.sum(lk, axis=-1, keepdims=True)


def _sb_prompt_kernel(q_ref, k_ref, v_ref, o_ref, qs_ref, *, tq, heads, blk):
    i = pl.program_id(2)
    rows = heads * tq
    later = _later_matrix(blk)
    _stack_heads(q_ref, qs_ref, heads, tq, HEAD_DIM)
    qpos = _tile_rows(i * tq + lax.broadcasted_iota(jnp.int32, (tq, 1), 0), heads)
    lane = lax.broadcasted_iota(jnp.int32, (1, blk), 1)
    last_block = (i * tq) // blk

    def block(start, masked, state):
        carry, acc = state
        k = k_ref[pl.ds(start, blk), :].astype(BF16)
        v = v_ref[pl.ds(start, blk), :].astype(BF16)
        neg_z = _dot_nt(qs_ref[...], k) * -(HEAD_DIM ** -0.5)
        a, carry = _stick_block(neg_z, (start + lane) < qpos if masked else None, later, carry)
        return carry, acc + _dot(a.astype(BF16), v)

    state = (jnp.zeros((rows, 1), F32), jnp.zeros((rows, HEAD_DIM), F32))
    state = block(pl.multiple_of(last_block * blk, blk), True, state)
    _, acc = lax.fori_loop(
        0, last_block, lambda n, st: block(pl.multiple_of((last_block - 1 - n) * blk, blk), False, st), state)
    for h in range(heads):
        o_ref[:, HEAD_DIM * h:HEAD_DIM * (h + 1)] = acc[tq * h:tq * (h + 1)].astype(o_ref.dtype)


def sb_prompt(proj, batch, t, h_c):
    tq = _pick(t, 128, 16)
    nq = t // tq
    heads = h_c // KVH_C
    blk = _pick(t, 256, LANES)
    assert blk % tq == 0, "a query tile must sit inside one key block"
    qcols = heads * HEAD_DIM
    kcol0 = h_c
    return pl.pallas_call(
        functools.partial(_sb_prompt_kernel, tq=tq, heads=heads, blk=blk),
        grid=(batch, KVH_C, nq),
        in_specs=[pl.BlockSpec((tq, qcols), lambda b, g, i: (b * nq + i, g)),
                  pl.BlockSpec((t, HEAD_DIM), lambda b, g, i: (b, kcol0 + g)),
                  pl.BlockSpec((t, HEAD_DIM), lambda b, g, i: (b, kcol0 + KVH_C + g))],
        out_specs=pl.BlockSpec((tq, qcols), lambda b, g, i: (b * nq + i, g)),
        out_shape=jax.ShapeDtypeStruct((batch * t, h_c * HEAD_DIM), BF16),
        scratch_shapes=[pltpu.VMEM((heads * tq, HEAD_DIM), BF16)],
        compiler_params=_params("parallel", "parallel", "parallel"),
        name="sb_prompt",
    )(proj, proj, proj)


def _page_specs(rows, cols, layer, pages_per_step, n_steps, reverse=False, n_seq=1):
    specs = []
    for g in range(n_seq):
        for r in range(pages_per_step):
            if reverse:
                imap = lambda b, j, pt, g=g, r=r: (layer, pt[b * n_seq + g, (n_steps - 1 - j) * pages_per_step + r],
                                                    0, 0)
            else:
                imap = lambda b, j, pt, g=g, r=r: (layer, pt[b * n_seq + g, j * pages_per_step + r], 0, 0)
            specs.append(pl.BlockSpec((None, None, rows, cols), imap))
    return specs


def _cat_lanes(refs, lo, hi):
    return jnp.concatenate([r[lo:hi, :] for r in refs], axis=1).astype(BF16)


def _cat_rows(refs, first, stride):
    return jnp.concatenate([r[pl.ds(first, PAGE_SIZE, stride=stride), :] for r in refs], axis=0).astype(BF16)


def _online_softmax(s, valid, v, state, v_feature_major=False):
    m_old, l_old, acc = state
    m_new = jnp.maximum(m_old, jnp.max(jnp.where(valid, s, NEG), axis=-1, keepdims=True))
    alpha = jnp.exp(m_old - m_new)
    p = jnp.where(valid, jnp.exp(s - m_new), 0.0)
    pv = _dot_nt(p.astype(BF16), v) if v_feature_major else _dot(p.astype(BF16), v)
    return m_new, alpha * l_old + jnp.sum(p, axis=-1, keepdims=True), alpha * acc + pv


def _init_softmax_state(m_ref, l_ref, acc_ref):
    m_ref[...] = jnp.full_like(m_ref, NEG)
    l_ref[...] = jnp.zeros_like(l_ref)
    acc_ref[...] = jnp.zeros_like(acc_ref)


def _idx_mla_decode_kernel(pt_ref, iq_ref, iw_ref, inew_ref, q_ref, lnew_ref, *rest, n_pg, n_seq):
    idx_pages = rest[:n_seq * n_pg]
    lat_pages = rest[n_seq * n_pg:2 * n_seq * n_pg]
    score_ref, snew_ref, ctx_ref, m_ref, l_ref, acc_ref = rest[2 * n_seq * n_pg:]
    j = pl.program_id(1)

    @pl.when(j == 0)
    def _():
        _init_softmax_state(m_ref, l_ref, acc_ref)

    scale = (D_NOPE + D_ROPE_B) ** -0.5
    for g in range(n_seq):
        iq = iq_ref[g]
        iw = iw_ref[g]
        keys_t = _cat_lanes(idx_pages[g * n_pg:(g + 1) * n_pg], 0, D_IDX)
        score_ref[g] = jnp.sum(iw * jnp.maximum(_dot(iq, keys_t), 0.0), axis=0, keepdims=True)
        snew_ref[g] = jnp.sum(iw * jnp.maximum(_dot(iq, inew_ref[g].astype(BF16)), 0.0), axis=0, keepdims=True)

        qn = q_ref[g, :, 0:KV_RANK]
        qr = q_ref[g, :, KV_RANK:KV_RANK + D_ROPE_B]
        pages = lat_pages[g * n_pg:(g + 1) * n_pg]
        c_t = _cat_lanes(pages, 0, KV_RANK)
        kr_t = _cat_lanes(pages, KV_RANK, KV_RANK + D_ROPE_B)
        s = (_dot(qn, c_t) + _dot(qr, kr_t)) * scale
        state = _online_softmax(s, jnp.full(s.shape, True), c_t, (m_ref[g], l_ref[g], acc_ref[g]), True)
        m_ref[g], l_ref[g], acc_ref[g] = state

    @pl.when(j == pl.num_programs(1) - 1)
    def _():
        for g in range(n_seq):
            qn = q_ref[g, :, 0:KV_RANK]
            qr = q_ref[g, :, KV_RANK:KV_RANK + D_ROPE_B]
            cn_t = lnew_ref[g, 0:KV_RANK, :].astype(BF16)
            krn_t = lnew_ref[g, KV_RANK:KV_RANK + D_ROPE_B, :].astype(BF16)
            sn = (_dot(qn, cn_t) + _dot(qr, krn_t)) * scale
            first = lax.broadcasted_iota(jnp.int32, sn.shape, 1) < 1
            _, l_fin, acc = _online_softmax(sn, first, cn_t, (m_ref[g], l_ref[g], acc_ref[g]), True)
            ctx_ref[g] = (acc / l_fin).astype(ctx_ref.dtype)


def idx_mla_decode(page_table, iq, iw, new_idx, pool_idx, q, new_lat, pool_lat, layer):
    b, n_pages = page_table.shape
    h = q.shape[1]
    width = KV_RANK + D_ROPE_B
    n_seq = _pick(b, 2, 1)
    n_pg = _pick(n_pages, 16, 1)
    n_steps = n_pages // n_pg
    per_seq = lambda *tail: pl.BlockSpec((n_seq,) + tail, lambda b_, j, pt: (b_,) + (0,) * len(tail))
    grid_spec = pltpu.PrefetchScalarGridSpec(
        num_scalar_prefetch=1,
        grid=(b // n_seq, n_steps),
        in_specs=[per_seq(H_IDX, D_IDX), per_seq(H_IDX, 1), per_seq(D_IDX, PAGE_SIZE),
                  per_seq(h, q.shape[2]), per_seq(width, PAGE_SIZE)]
        + _page_specs(D_IDX, PAGE_SIZE, layer, n_pg, n_steps, n_seq=n_seq)
        + _page_specs(width, PAGE_SIZE, layer, n_pg, n_steps, n_seq=n_seq),
        out_specs=[pl.BlockSpec((n_seq, 1, n_pg * PAGE_SIZE), lambda b_, j, pt: (b_, 0, j)),
                   per_seq(1, PAGE_SIZE), per_seq(h, KV_RANK)],
        scratch_shapes=[pltpu.VMEM((n_seq, h, 1), F32), pltpu.VMEM((n_seq, h, 1), F32),
                        pltpu.VMEM((n_seq, h, KV_RANK), F32)],
    )
    return pl.pallas_call(
        functools.partial(_idx_mla_decode_kernel, n_pg=n_pg, n_seq=n_seq),
        grid_spec=grid_spec,
        out_shape=[jax.ShapeDtypeStruct((b, 1, n_pages * PAGE_SIZE), F32),
                   jax.ShapeDtypeStruct((b, 1, PAGE_SIZE), F32),
                   jax.ShapeDtypeStruct((b, h, KV_RANK), BF16)],
        compiler_params=_params("parallel", "arbitrary"),
        name="idx_mla_decode",
    )(page_table, iq, iw, new_idx, q, new_lat, *([pool_idx] * (n_seq * n_pg)), *([pool_lat] * (n_seq * n_pg)))


def _select_decode_kernel(score_ref, sel_ref, *, n_valid, topk):
    score = score_ref[...]
    kpos = lax.broadcasted_iota(jnp.int32, (1, score.shape[1]), 1)
    key = _sort_key(jnp.where(kpos < n_valid, score, -jnp.inf))

    def write(c, lo, hi, value):
        sel_ref[:, lo:hi] = value

    _select_topk([key], topk, write, LANES)


def select_decode(score, n_valid, topk):
    b, n = score.shape
    tb = _pick(b, 32, 8)
    return pl.pallas_call(
        functools.partial(_select_decode_kernel, n_valid=n_valid, topk=topk),
        grid=(b // tb,),
        in_specs=[pl.BlockSpec((tb, n), lambda i: (i, 0))],
        out_specs=pl.BlockSpec((tb, n), lambda i: (i, 0)),
        out_shape=jax.ShapeDtypeStruct((b, n), F32),
        compiler_params=_params("parallel"),
        name="select_decode",
    )(score)


def _dsa_decode_kernel(pt_ref, q_ref, sel_ref, selnew_ref, new_ref, *rest, n_pg, n_seq):
    pages = rest[:n_seq * n_pg]
    o_ref, m_ref, l_ref, acc_ref = rest[n_seq * n_pg:]
    j = pl.program_id(1)

    @pl.when(j == 0)
    def _():
        _init_softmax_state(m_ref, l_ref, acc_ref)

    scale = HEAD_DIM ** -0.5
    for g in range(n_seq):
        k = _cat_rows(pages[g * n_pg:(g + 1) * n_pg], 0, 2)
        v = _cat_rows(pages[g * n_pg:(g + 1) * n_pg], 1, 2)
        s = _dot_nt(q_ref[g], k) * scale
        m_ref[g], l_ref[g], acc_ref[g] = _online_softmax(s, sel_ref[g] > 0.5, v, (m_ref[g], l_ref[g], acc_ref[g]))

    @pl.when(j == pl.num_programs(1) - 1)
    def _():
        for g in range(n_seq):
            kn = new_ref[g, pl.ds(0, PAGE_SIZE, stride=2), :].astype(BF16)
            vn = new_ref[g, pl.ds(1, PAGE_SIZE, stride=2), :].astype(BF16)
            sn = _dot_nt(q_ref[g], kn) * scale
            _, l_fin, acc = _online_softmax(sn, selnew_ref[g] > 0.5, vn, (m_ref[g], l_ref[g], acc_ref[g]))
            o_ref[g] = (acc / l_fin).astype(o_ref.dtype)


def dsa_decode(page_table, q, sel, new_page, pool, layer):
    b, n_pages = page_table.shape
    h = q.shape[1]
    n_seq = _pick(b, 2, 1)
    n_pg = _pick(n_pages, 16, 1)
    n_steps = n_pages // n_pg
    grid_spec = pltpu.PrefetchScalarGridSpec(
        num_scalar_prefetch=1,
        grid=(b // n_seq, n_steps),
        in_specs=[pl.BlockSpec((n_seq, h, HEAD_DIM), lambda b_, j, pt: (b_, 0, 0)),
                  pl.BlockSpec((n_seq, 1, n_pg * PAGE_SIZE), lambda b_, j, pt: (b_, 0, j)),
                  pl.BlockSpec((n_seq, 1, PAGE_SIZE), lambda b_, j, pt: (b_, 0, n_pages)),
                  pl.BlockSpec((n_seq, 2 * PAGE_SIZE, HEAD_DIM), lambda b_, j, pt: (b_, 0, 0))]
        + _page_specs(2 * PAGE_SIZE, HEAD_DIM, layer, n_pg, n_steps, n_seq=n_seq),
        out_specs=pl.BlockSpec((n_seq, h, HEAD_DIM), lambda b_, j, pt: (b_, 0, 0)),
        scratch_shapes=[pltpu.VMEM((n_seq, h, 1), F32), pltpu.VMEM((n_seq, h, 1), F32),
                        pltpu.VMEM((n_seq, h, HEAD_DIM), F32)],
    )
    return pl.pallas_call(
        functools.partial(_dsa_decode_kernel, n_pg=n_pg, n_seq=n_seq),
        grid_spec=grid_spec,
        out_shape=jax.ShapeDtypeStruct((b, h, HEAD_DIM), BF16),
        compiler_params=_params("parallel", "arbitrary"),
        name="dsa_decode",
    )(page_table, q, sel, sel, new_page, *([pool] * (n_seq * n_pg)))


def _sb_decode_kernel(pt_ref, q_ref, *rest, n_pg, blk):
    pages = rest[:n_pg]
    o_ref, carry_ref, acc_ref = rest[n_pg:]
    j = pl.program_id(1)

    @pl.when(j == 0)
    def _():
        carry_ref[...] = jnp.zeros_like(carry_ref)
        acc_ref[...] = jnp.zeros_like(acc_ref)

    later = _later_matrix(blk)
    heads = q_ref.shape[1]
    n_blocks = n_pg * PAGE_SIZE // blk
    z_blocks, values = [], []
    for g in range(KVH_C):
        z = _dot_nt(q_ref[g], _cat_rows(pages, g, 2 * KVH_C)) * -(HEAD_DIM ** -0.5)
        z_blocks += [z[:, blk * n:blk * (n + 1)] for n in range(n_blocks)]
        values.append(_cat_rows(pages, KVH_C + g, 2 * KVH_C))
    neg_z = jnp.concatenate(z_blocks, axis=0)
    rows = neg_z.shape[0]
    lk = _log_keep(neg_z)
    hi = lk.astype(BF16)
    lo = (lk - hi.astype(F32)).astype(BF16)
    within = _dot(jnp.concatenate([hi, lo], axis=0), later)
    total = jnp.sum(lk, axis=-1, keepdims=True)
    carries = [None] * (KVH_C * n_blocks)
    for g in range(KVH_C):
        running = carry_ref[g]
        for n in reversed(range(n_blocks)):
            b = g * n_blocks + n
            carries[b] = running
            running = running + total[heads * b:heads * (b + 1)]
        carry_ref[g] = running
    tail = within[:rows] + within[rows:] + jnp.concatenate(carries, axis=0)
    a = jnp.exp(lk - neg_z + tail).astype(BF16)
    for g in range(KVH_C):
        a_g = jnp.concatenate([a[heads * (g * n_blocks + n):heads * (g * n_blocks + n + 1)]
                               for n in range(n_blocks)], axis=1)
        acc_ref[g] = acc_ref[g] + _dot(a_g, values[g])

    @pl.when(j == pl.num_programs(1) - 1)
    def _():
        o_ref[...] = acc_ref[...].astype(o_ref.dtype)


def sb_decode(page_table, q, pool, layer):
    b, n_pages = page_table.shape
    heads = q.shape[2]
    n_pg = _pick(n_pages, 16, 1)
    n_steps = n_pages // n_pg
    blk = _pick(n_pg * PAGE_SIZE, 256, LANES)
    grid_spec = pltpu.PrefetchScalarGridSpec(
        num_scalar_prefetch=1,
        grid=(b, n_steps),
        in_specs=[pl.BlockSpec((None, KVH_C, heads, HEAD_DIM), lambda b_, j, pt: (b_, 0, 0, 0))]
        + _page_specs(2 * KVH_C * PAGE_SIZE, HEAD_DIM, layer, n_pg, n_steps, True),
        out_specs=pl.BlockSpec((None, KVH_C, heads, HEAD_DIM), lambda b_, j, pt: (b_, 0, 0, 0)),
        scratch_shapes=[pltpu.VMEM((KVH_C, heads, 1), F32), pltpu.VMEM((KVH_C, heads, HEAD_DIM), F32)],
    )
    return pl.pallas_call(
        functools.partial(_sb_decode_kernel, n_pg=n_pg, blk=blk),
        grid_spec=grid_spec,
        out_shape=jax.ShapeDtypeStruct((b, KVH_C, heads, HEAD_DIM), BF16),
        compiler_params=_params("parallel", "arbitrary"),
        name="sb_decode",
    )(page_table, q, *([pool] * n_pg))


def _pad_cols(w, width):
    return jnp.pad(w, ((0, 0), (0, width - w.shape[1])))


def _pad_row(g, width):
    return jnp.pad(g, (0, width - g.shape[0])).reshape(1, width)


def _new_page_rows(rows, rows_per_pos):
    b = rows.shape[0]
    r = rows.reshape(b, rows_per_pos, -1)
    return jnp.pad(r, ((0, 0), (0, rows_per_pos * (PAGE_SIZE - 1)), (0, 0)))


def _new_page_feature_major(rows):
    return jnp.pad(rows[:, :, None], ((0, 0), (0, 0), (0, PAGE_SIZE - 1)))


def even_layer(x, tabs, n_prompt, batch, t, page_table, layer, cache_a_kv, cache_a_idx, cache_b_latent,
               e_norm, e_w_in, a_qnorm, a_knorm, idx_knorm, b_qa_norm, b_wq_b, b_kv_norm, b_wkv_b,
               b_qnorm, b_krnorm, e_w_out):
    m, d = x.shape
    h_a = d // 2 // HEAD_DIM
    h_b = h_a
    past_len = page_table.shape[1] * PAGE_SIZE
    dec_b = m - n_prompt

    widths = (h_a * HEAD_DIM, HEAD_DIM, HEAD_DIM, H_IDX * D_IDX, H_IDX, D_IDX, Q_RANK, KV_RANK, D_ROPE_B)
    src = np.cumsum((0,) + (h_a * HEAD_DIM, HEAD_DIM, HEAD_DIM, H_IDX * D_IDX, H_IDX, D_IDX, Q_RANK, KV_RANK))
    padded = tuple(-(-w // LANES) * LANES for w in widths)
    offs = tuple(int(o) for o in np.cumsum((0,) + padded)[:-1])
    total = -(-sum(padded) // 512) * 512
    cols = [_pad_cols(e_w_in[:, int(s):int(s) + w], p) for s, w, p in zip(src, widths, padded)]
    w_in = _pad_cols(jnp.concatenate(cols, axis=1), total).astype(BF16)

    h = rms_norm_rows(x, e_norm)
    proj = matmul(h, w_in)
    gains = (a_qnorm.reshape(1, -1), a_knorm.reshape(1, -1), _pad_row(idx_knorm, LANES),
             b_qa_norm.reshape(1, -1), b_kv_norm.reshape(1, -1), _pad_row(b_krnorm, LANES))
    qa, kv, iq, iw, ik, ika, ikb, qlo, lat = even_post(proj, tabs, gains, h_a, offs)

    wq = b_wq_b.reshape(Q_RANK, h_b, D_NOPE + D_ROPE_B)
    wq_nope = wq[:, :, :D_NOPE].reshape(Q_RANK, h_b * D_NOPE)
    wq_rope = jnp.pad(wq[:, :, D_NOPE:], ((0, 0), (0, 0), (0, LANES - D_ROPE_B))).reshape(Q_RANK, h_b * LANES)
    qb = matmul(qlo, jnp.concatenate([wq_nope, wq_rope], axis=1).astype(BF16))
    wuk = jnp.transpose(b_wkv_b[:, :, :D_NOPE], (1, 2, 0)).astype(BF16)
    wuv = jnp.transpose(b_wkv_b[:, :, D_NOPE:], (1, 0, 2)).astype(BF16)
    qlat = mla_q(qb, tabs, b_qnorm[:D_NOPE].reshape(1, -1), _pad_row(b_qnorm[D_NOPE:], LANES), wuk, h_b)

    topk_p = min(TOPK_MAX, t // 4)
    oa_p = dsa_prompt(iq, iw, ika, ikb, qa, kv, batch, t, h_a, topk_p)
    ob_p = mla_prompt(qlat, lat, wuv, batch, t, h_b)

    topk_s = min(TOPK_MAX, (past_len + 1) // 4)
    pool_kv = cache_a_kv.reshape(cache_a_kv.shape[:2] + (2 * PAGE_SIZE, HEAD_DIM))
    pool_idx = jnp.swapaxes(cache_a_idx, 2, 3)
    pool_lat = jnp.swapaxes(cache_b_latent, 2, 3)
    iq_s = iq[n_prompt:].reshape(dec_b, H_IDX, D_IDX)
    iw_s = iw[n_prompt:, :H_IDX].reshape(dec_b, H_IDX, 1)
    sc_old, sc_new, ctx_s = idx_mla_decode(
        page_table, iq_s, iw_s, _new_page_feature_major(ik[n_prompt:]), pool_idx,
        qlat[n_prompt:].reshape(dec_b, h_b, KV_RANK + LANES), _new_page_feature_major(lat[n_prompt:]), pool_lat, layer)
    sel = select_decode(jnp.concatenate([sc_old[:, 0], sc_new[:, 0]], axis=1), past_len + 1, topk_s)
    oa_s = dsa_decode(page_table, qa[n_prompt:].reshape(dec_b, h_a, HEAD_DIM), sel[:, None, :],
                      _new_page_rows(kv[n_prompt:], 2), pool_kv, layer)
    ob_s = head_out(ctx_s.reshape(dec_b, h_b * KV_RANK), wuv)

    mix_in = jnp.concatenate([jnp.concatenate([oa_p, ob_p], axis=1),
                              jnp.concatenate([oa_s.reshape(dec_b, -1), ob_s], axis=1)], axis=0)
    x = matmul_residual(mix_in, e_w_out.astype(BF16), x, 1.0, tm_target=832)
    return x, kv, ik, lat


def odd_layer(x, n_prompt, batch, t, page_table, layer, cache_c_kv, o_norm, o_w_in, o_w_out):
    m, d = x.shape
    h_c = d // HEAD_DIM
    dec_b = m - n_prompt
    h = rms_norm_rows(x, o_norm)
    proj = matmul(h, o_w_in.astype(BF16))
    o_p = sb_prompt(proj, batch, t, h_c)
    q_s = proj[n_prompt:, :h_c * HEAD_DIM].astype(BF16).reshape(dec_b, KVH_C, h_c // KVH_C, HEAD_DIM)
    pool = cache_c_kv.reshape(cache_c_kv.shape[:2] + (2 * KVH_C * PAGE_SIZE, HEAD_DIM))
    o_s = sb_decode(page_table, q_s, pool, layer)
    mix_in = jnp.concatenate([o_p, o_s.reshape(dec_b, -1)], axis=0)
    x = matmul_residual(mix_in, o_w_out.astype(BF16), x, 1.0, tm_target=832)
    return x, proj[:, h_c * HEAD_DIM:]


def kernel(x_prompt, x_sample, cache_a_kv, cache_a_idx, cache_b_latent, cache_c_kv, page_table, ffn1_norm, ffn1_w_in, ffn1_w_out, ffn2_norm, ffn2_w_in, ffn2_w_out, e_norm, e_w_in, e_a_qnorm, e_a_knorm, e_idx_knorm, e_b_qa_norm, e_b_wq_b, e_b_kv_norm, e_b_wkv_b, e_b_qnorm, e_b_krnorm, e_w_out, o_norm, o_w_in, o_w_out):
    batch, t, d = x_prompt.shape
    dec_b, dec_t, _ = x_sample.shape
    assert dec_t == 1, "one new token per sampled sequence"
    depth = ffn1_norm.shape[0]
    n_prompt = batch * t
    past_len = page_table.shape[1] * PAGE_SIZE
    x = jnp.concatenate([x_prompt.reshape(n_prompt, d), x_sample.reshape(dec_b, d)], axis=0)

    pos = jnp.concatenate([jnp.tile(jnp.arange(t, dtype=F32), batch),
                           jnp.full((dec_b,), past_len, F32)]).reshape(-1, 1)
    rows = jnp.stack([_rope_rows(HEAD_DIM // 4, HEAD_DIM), _rope_rows(D_IDX // 4, D_IDX),
                      _rope_rows(D_ROPE_B, LANES)])
    tabs = rope_tables(pos, rows)

    w1_in, w1_out = ffn1_w_in.astype(BF16), ffn1_w_out.astype(BF16)
    w2_in, w2_out = ffn2_w_in.astype(BF16), ffn2_w_out.astype(BF16)
    akv, aidx, blat, ckv = [], [], [], []
    for layer in range(depth):
        x = ffn_half_step(x, ffn1_norm[layer], w1_in, w1_out, layer)
        if layer % 2 == 0:
            e = layer // 2
            x, kv, ik, lat = even_layer(
                x, tabs, n_prompt, batch, t, page_table, e, cache_a_kv, cache_a_idx, cache_b_latent,
                e_norm[e], e_w_in[e], e_a_qnorm[e], e_a_knorm[e], e_idx_knorm[e], e_b_qa_norm[e], e_b_wq_b[e],
                e_b_kv_norm[e], e_b_wkv_b[e], e_b_qnorm[e], e_b_krnorm[e], e_w_out[e])
            akv.append(kv)
            aidx.append(ik)
            blat.append(lat)
        else:
            o = layer // 2
            x, kvc = odd_layer(x, n_prompt, batch, t, page_table, o, cache_c_kv, o_norm[o], o_w_in[o], o_w_out[o])
            ckv.append(kvc)
        x = ffn_half_step(x, ffn2_norm[layer], w2_in, w2_out, layer,
                          split_at=n_prompt if layer == depth - 1 else None)
    y_prompt, y_sample = x

    def split(rows_list, tail):
        a = jnp.stack(rows_list)
        return (a[:, :n_prompt].reshape((len(rows_list), batch, t) + tail),
                a[:, n_prompt:].reshape((len(rows_list), dec_b, dec_t) + tail))

    akv_p, akv_s = split(akv, (2, 1, HEAD_DIM))
    aidx_p, aidx_s = split(aidx, (D_IDX,))
    blat_p, blat_s = split(blat, (KV_RANK + D_ROPE_B,))
    ckv_p, ckv_s = split(ckv, (2, KVH_C, HEAD_DIM))
    return (y_prompt.reshape(batch, t, d), y_sample.reshape(dec_b, dec_t, d),
            akv_p, akv_s, aidx_p, aidx_s, blat_p, blat_s, ckv_p, ckv_s)
```

```python
import functools

import numpy as np
import jax
import jax.numpy as jnp
from jax import lax
from jax.experimental import pallas as pl
from jax.experimental.pallas import tpu as pltpu

HEAD_DIM = 128
ROPE_THETA = 500000.0
EPS = 1e-6
PAGE_SIZE = 128
H_IDX = 32
D_IDX = 64
TOPK_MAX = 256
Q_RANK = 768
KV_RANK = 256
D_NOPE = 128
D_ROPE_B = 64
D_VB = 128
KVH_C = 2
LANES = 128
V7X_VMEM_LIMIT = 56 * 1024 * 1024
NEG = -1e30
INT_MIN = -(2 ** 31)
NEG_INF_KEY = INT_MIN + 0x7FFFFF

BF16 = jnp.bfloat16
F32 = jnp.float32


def _pick(n, target, mult):
    best = None
    for d in range(mult, min(n, target) + 1, mult):
        if n % d == 0:
            best = d
    assert best is not None, (n, target, mult)
    return best


def _params(*sem):
    return pltpu.CompilerParams(dimension_semantics=sem, vmem_limit_bytes=V7X_VMEM_LIMIT)


def _dot(a, b):
    return jnp.dot(a, b, preferred_element_type=F32)


def _dot_nt(a, b):
    return lax.dot_general(a, b, (((1,), (1,)), ((), ())), preferred_element_type=F32)


def _norm_kernel(x_ref, g_ref, o_ref):
    x = x_ref[...]
    y = x * lax.rsqrt(jnp.mean(x * x, axis=-1, keepdims=True) + EPS)
    o_ref[...] = (y * g_ref[...]).astype(o_ref.dtype)


def rms_norm_rows(x, g):
    m, d = x.shape
    tm = _pick(m, 512, 16)
    return pl.pallas_call(
        _norm_kernel,
        grid=(m // tm,),
        in_specs=[pl.BlockSpec((tm, d), lambda i: (i, 0)), pl.BlockSpec((1, d), lambda i: (0, 0))],
        out_specs=pl.BlockSpec((tm, d), lambda i: (i, 0)),
        out_shape=jax.ShapeDtypeStruct((m, d), BF16),
        compiler_params=_params("parallel"),
        name="rms_norm_rows",
    )(x, g.reshape(1, d))


def _mm_kernel(a_ref, w_ref, o_ref):
    o_ref[...] = _dot(a_ref[...], w_ref[...]).astype(o_ref.dtype)


def matmul(a, w, out_dtype=F32, tm_target=1664, tn_target=512):
    m, k = a.shape
    n = w.shape[1]
    tm = _pick(m, tm_target, 16)
    tn = _pick(n, tn_target, LANES)
    return pl.pallas_call(
        _mm_kernel,
        grid=(m // tm, n // tn),
        in_specs=[pl.BlockSpec((tm, k), lambda i, j: (i, 0)), pl.BlockSpec((k, tn), lambda i, j: (0, j))],
        out_specs=pl.BlockSpec((tm, tn), lambda i, j: (i, j)),
        out_shape=jax.ShapeDtypeStruct((m, n), out_dtype),
        compiler_params=_params("parallel", "parallel"),
        name="matmul",
    )(a, w)


def _mm_res_kernel(a_ref, w_ref, r_ref, o_ref, *, scale):
    o_ref[...] = r_ref[...] + scale * _dot(a_ref[...], w_ref[...])


def _weight_spec(w, layer, tn, col_block0=0):
    if w.ndim == 2:
        return pl.BlockSpec((w.shape[0], tn), lambda i, j: (0, j + col_block0))
    return pl.BlockSpec((None, w.shape[1], tn), lambda i, j: (layer, 0, j + col_block0))


def matmul_residual(a, w, res, scale, layer=None, tm_target=416, tn_target=512, row_start=0, n_rows=None):
    k = a.shape[1]
    n = w.shape[-1]
    n_rows = a.shape[0] if n_rows is None else n_rows
    tm = _pick(int(np.gcd(row_start, n_rows)), tm_target, 16)
    tn = _pick(n, tn_target, LANES)
    i0 = row_start // tm
    return pl.pallas_call(
        functools.partial(_mm_res_kernel, scale=scale),
        grid=(n_rows // tm, n // tn),
        in_specs=[pl.BlockSpec((tm, k), lambda i, j: (i + i0, 0)), _weight_spec(w, layer, tn),
                  pl.BlockSpec((tm, tn), lambda i, j: (i + i0, j))],
        out_specs=pl.BlockSpec((tm, tn), lambda i, j: (i, j)),
        out_shape=jax.ShapeDtypeStruct((n_rows, n), F32),
        compiler_params=_params("parallel", "parallel"),
        name="matmul_residual",
    )(a, w, res)


def _ffn_in_kernel(a_ref, wg_ref, wu_ref, o_ref):
    a = a_ref[...]
    g = _dot(a, wg_ref[...].astype(BF16))
    u = _dot(a, wu_ref[...].astype(BF16))
    o_ref[...] = (g * jax.nn.sigmoid(g) * u).astype(o_ref.dtype)


def ffn_in(h, w_in, layer, tm_target=1040, tn_target=256):
    m, k = h.shape
    f = w_in.shape[-1] // 2
    tm = _pick(m, tm_target, 16)
    tn = _pick(f, tn_target, LANES)
    nj = f // tn
    return pl.pallas_call(
        _ffn_in_kernel,
        grid=(m // tm, nj),
        in_specs=[pl.BlockSpec((tm, k), lambda i, j: (i, 0)), _weight_spec(w_in, layer, tn),
                  _weight_spec(w_in, layer, tn, nj)],
        out_specs=pl.BlockSpec((tm, tn), lambda i, j: (i, j)),
        out_shape=jax.ShapeDtypeStruct((m, f), BF16),
        compiler_params=_params("parallel", "parallel"),
        name="ffn_in",
    )(h, w_in, w_in)


def ffn_half_step(x, g, w_in, w_out, layer, split_at=None):
    h = rms_norm_rows(x, g)
    act = ffn_in(h, w_in, layer)
    if split_at is None:
        return matmul_residual(act, w_out, x, 0.5, layer)
    head = matmul_residual(act, w_out, x, 0.5, layer, tm_target=512, tn_target=256, n_rows=split_at)
    tail = matmul_residual(act, w_out, x, 0.5, layer, row_start=split_at, n_rows=x.shape[0] - split_at)
    return head, tail


def _rope_rows(rot, width):
    half = rot // 2
    j = np.arange(LANES) % width
    in_rot = (j < rot).astype(np.float32)
    take_lo = ((j >= half) & (j < rot)).astype(np.float32)
    take_hi = (j < half).astype(np.float32)
    inv = ROPE_THETA ** (-jnp.arange(half, dtype=F32) / half)
    inv_row = jnp.where(jnp.asarray(j < rot), inv[j % half], 0.0).astype(F32)
    return jnp.stack([inv_row, jnp.asarray(in_rot), jnp.asarray(take_lo), jnp.asarray(take_hi)])


def _rope_table_kernel(pos_ref, rows_ref, o_ref):
    pos = pos_ref[...]
    for f in range(rows_ref.shape[0]):
        rows = rows_ref[f]
        ang = pos * rows[0:1]
        cos = jnp.cos(ang)
        sin = jnp.sin(ang)
        o_ref[f, 0] = jnp.where(rows[1:2] > 0.5, cos, 1.0)
        o_ref[f, 1] = rows[2:3] * sin
        o_ref[f, 2] = -rows[3:4] * sin


def rope_tables(pos, rows):
    m = pos.shape[0]
    nf = rows.shape[0]
    tm = _pick(m, 512, 8)
    return pl.pallas_call(
        _rope_table_kernel,
        grid=(m // tm,),
        in_specs=[pl.BlockSpec((tm, 1), lambda i: (i, 0)), pl.BlockSpec((nf, 4, LANES), lambda i: (0, 0, 0))],
        out_specs=pl.BlockSpec((nf, 3, tm, LANES), lambda i: (0, 0, i, 0)),
        out_shape=jax.ShapeDtypeStruct((nf, 3, m, LANES), F32),
        compiler_params=_params("parallel"),
        name="rope_tables",
    )(pos, rows)


def _rope(x, tab, half):
    return x * tab[0] + pltpu.roll(x, half, 1) * tab[1] + pltpu.roll(x, LANES - half, 1) * tab[2]


def _head_norm(x, g, n):
    return x * lax.rsqrt(jnp.sum(x * x, axis=-1, keepdims=True) / n + EPS) * g


def _even_post_kernel(p_ref, tab_ref, gq_ref, gk_ref, gik_ref, gqa_ref, gkv_ref, gkr_ref,
                      qa_o, kv_o, iq_o, iw_o, ik_o, ika_o, ikb_o, qlo_o, lat_o, *, h_a, offs):
    o_qa, o_ka, o_va, o_iq, o_iw, o_ik, o_qlo, o_c, o_kr = offs
    tab_a = tab_ref[0]
    tab_i = tab_ref[1]
    tab_b = tab_ref[2]
    rot_a = HEAD_DIM // 4
    rot_i = D_IDX // 4
    for h in range(h_a):
        x = p_ref[:, o_qa + LANES * h:o_qa + LANES * (h + 1)]
        y = _rope(_head_norm(x, gq_ref[...], HEAD_DIM), tab_a, rot_a // 2)
        qa_o[:, LANES * h:LANES * (h + 1)] = y.astype(qa_o.dtype)
    ka = _rope(_head_norm(p_ref[:, o_ka:o_ka + LANES], gk_ref[...], HEAD_DIM), tab_a, rot_a // 2)
    kv_o[:, 0:LANES] = ka
    kv_o[:, LANES:2 * LANES] = p_ref[:, o_va:o_va + LANES]
    for b in range(H_IDX * D_IDX // LANES):
        x = p_ref[:, o_iq + LANES * b:o_iq + LANES * (b + 1)]
        iq_o[:, LANES * b:LANES * (b + 1)] = _rope(x, tab_i, rot_i // 2).astype(iq_o.dtype)
    iw_o[...] = p_ref[:, o_iw:o_iw + LANES] * (H_IDX * D_IDX) ** -0.5
    ik = _rope(_head_norm(p_ref[:, o_ik:o_ik + LANES], gik_ref[...], D_IDX), tab_i, rot_i // 2)
    ik_o[...] = ik[:, :D_IDX]
    ika_o[...] = ik.astype(ika_o.dtype)
    ikb_o[...] = pltpu.roll(ik, D_IDX, 1).astype(ikb_o.dtype)
    qlo = p_ref[:, o_qlo:o_qlo + Q_RANK]
    qlo_o[...] = _head_norm(qlo, gqa_ref[...], Q_RANK).astype(qlo_o.dtype)
    lat_o[:, 0:KV_RANK] = _head_norm(p_ref[:, o_c:o_c + KV_RANK], gkv_ref[...], KV_RANK)
    kr = _rope(_head_norm(p_ref[:, o_kr:o_kr + LANES], gkr_ref[...], D_ROPE_B), tab_b, D_ROPE_B // 2)
    lat_o[:, KV_RANK:KV_RANK + D_ROPE_B] = kr[:, :D_ROPE_B]


def even_post(proj, tabs, gains, h_a, offs):
    m, npj = proj.shape
    tm = _pick(m, 320, 16)
    row = lambda w: pl.BlockSpec((tm, w), lambda i: (i, 0))
    full = lambda a: pl.BlockSpec(a.shape, lambda i: (0,) * a.ndim)
    out_shape = [
        jax.ShapeDtypeStruct((m, h_a * HEAD_DIM), BF16),
        jax.ShapeDtypeStruct((m, 2 * HEAD_DIM), F32),
        jax.ShapeDtypeStruct((m, H_IDX * D_IDX), BF16),
        jax.ShapeDtypeStruct((m, LANES), F32),
        jax.ShapeDtypeStruct((m, D_IDX), F32),
        jax.ShapeDtypeStruct((m, LANES), BF16),
        jax.ShapeDtypeStruct((m, LANES), BF16),
        jax.ShapeDtypeStruct((m, Q_RANK), BF16),
        jax.ShapeDtypeStruct((m, KV_RANK + D_ROPE_B), F32),
    ]
    return pl.pallas_call(
        functools.partial(_even_post_kernel, h_a=h_a, offs=offs),
        grid=(m // tm,),
        in_specs=[row(npj), pl.BlockSpec((3, 3, tm, LANES), lambda i: (0, 0, i, 0))] + [full(g) for g in gains],
        out_specs=[row(s.shape[1]) for s in out_shape],
        out_shape=out_shape,
        compiler_params=_params("parallel"),
        name="even_post",
    )(proj, tabs, *gains)


def _mla_q_kernel(qb_ref, tab_ref, gn_ref, gr_ref, wuk_ref, o_ref, *, h_b):
    tab_b = tab_ref[0]
    qw = KV_RANK + LANES
    for h in range(h_b):
        xn = qb_ref[:, LANES * h:LANES * (h + 1)]
        xr = qb_ref[:, LANES * (h_b + h):LANES * (h_b + h + 1)]
        ss = jnp.sum(xn * xn, axis=-1, keepdims=True) + jnp.sum(xr * xr, axis=-1, keepdims=True)
        r = lax.rsqrt(ss / (D_NOPE + D_ROPE_B) + EPS)
        yn = xn * r * gn_ref[...]
        yr = _rope(xr * r * gr_ref[...], tab_b, D_ROPE_B // 2)
        o_ref[:, qw * h:qw * h + KV_RANK] = _dot(yn.astype(BF16), wuk_ref[h]).astype(o_ref.dtype)
        o_ref[:, qw * h + KV_RANK:qw * (h + 1)] = yr.astype(o_ref.dtype)


def mla_q(qb, tabs, gn, gr, wuk, h_b):
    m = qb.shape[0]
    tm = _pick(m, 320, 16)
    qw = KV_RANK + LANES
    return pl.pallas_call(
        functools.partial(_mla_q_kernel, h_b=h_b),
        grid=(m // tm,),
        in_specs=[pl.BlockSpec((tm, qb.shape[1]), lambda i: (i, 0)),
                  pl.BlockSpec((1, 3, tm, LANES), lambda i: (2, 0, i, 0)),
                  pl.BlockSpec((1, LANES), lambda i: (0, 0)), pl.BlockSpec((1, LANES), lambda i: (0, 0)),
                  pl.BlockSpec(wuk.shape, lambda i: (0, 0, 0))],
        out_specs=pl.BlockSpec((tm, h_b * qw), lambda i: (i, 0)),
        out_shape=jax.ShapeDtypeStruct((m, h_b * qw), BF16),
        compiler_params=_params("parallel"),
        name="mla_q",
    )(qb, tabs, gn, gr, wuk)


def _head_out_kernel(c_ref, w_ref, o_ref, *, n_heads):
    kw = w_ref.shape[1]
    nw = w_ref.shape[2]
    for h in range(n_heads):
        o_ref[:, nw * h:nw * (h + 1)] = _dot(c_ref[:, kw * h:kw * (h + 1)], w_ref[h]).astype(o_ref.dtype)


def head_out(c, w):
    m = c.shape[0]
    n_heads, kw, nw = w.shape
    return pl.pallas_call(
        functools.partial(_head_out_kernel, n_heads=n_heads),
        grid=(1,),
        in_specs=[pl.BlockSpec(c.shape, lambda i: (0, 0)), pl.BlockSpec(w.shape, lambda i: (0, 0, 0))],
        out_specs=pl.BlockSpec((m, n_heads * nw), lambda i: (0, 0)),
        out_shape=jax.ShapeDtypeStruct((m, n_heads * nw), BF16),
        compiler_params=_params("arbitrary"),
        name="head_out",
    )(c, w)


def _sort_key(score):
    bits = pltpu.bitcast(score + 0.0, jnp.int32)
    return bits ^ ((bits >> 31) & 0x7FFFFFFF)


def _count(keys, pred):
    return sum(jnp.sum(jnp.where(pred(key), 1.0, 0.0), axis=-1, keepdims=True) for key in keys)


def _kth_largest_key(keys, k):
    rows = keys[0].shape[0]

    def body(i, t):
        cand = t | jnp.left_shift(jnp.int32(1), 31 - i)
        cnt = _count(keys, lambda key: key >= (cand ^ INT_MIN))
        return jnp.where(cnt >= k, cand, t)

    t = lax.fori_loop(0, 32, body, jnp.zeros((rows, 1), jnp.int32))
    return t ^ INT_MIN


def _select_topk(keys, k, write, blk):
    rows = keys[0].shape[0]
    thr = _kth_largest_key(keys, k)
    need = k - _count(keys, lambda key: key > thr)
    ri = lax.broadcasted_iota(jnp.int32, (blk, blk), 0)
    ci = lax.broadcasted_iota(jnp.int32, (blk, blk), 1)
    earlier = jnp.where(ri < ci, 1.0, 0.0).astype(BF16)
    seen = jnp.zeros((rows, 1), F32)
    for c, key in enumerate(keys):
        for b in range(key.shape[1] // blk):
            kb = key[:, blk * b:blk * (b + 1)]
            e = jnp.where(kb == thr, 1.0, 0.0)
            rank = _dot(e.astype(BF16), earlier) + seen
            seen = seen + jnp.sum(e, axis=-1, keepdims=True)
            take = jnp.where(rank < need, e, 0.0)
            write(c, blk * b, blk * (b + 1), jnp.where(kb > thr, 1.0, take))


def _stack_heads(src_ref, dst_ref, heads, tq, width):
    for h in range(heads):
        dst_ref[tq * h:tq * (h + 1), :] = src_ref[:, width * h:width * (h + 1)].astype(dst_ref.dtype)


def _tile_rows(x, reps):
    return jnp.concatenate([x] * reps, axis=0)


def _dsa_prompt_kernel(iq_ref, iw_ref, ika_ref, ikb_ref, qa_ref, kv_ref, o_ref,
                       iqs_ref, qas_ref, key_ref, sel_ref, m_ref, l_ref, acc_ref, *, tq, h_a, topk, ck):
    i = pl.program_id(1)
    n_chunks = kv_ref.shape[0] // ck
    n_pairs = H_IDX // 2
    _stack_heads(iq_ref, iqs_ref, n_pairs, tq, LANES)
    _stack_heads(qa_ref, qas_ref, h_a, tq, HEAD_DIM)
    qpos = i * tq + lax.broadcasted_iota(jnp.int32, (tq, 1), 0)
    lane = lax.broadcasted_iota(jnp.int32, (1, ck), 1)
    last = i * tq + tq - 1
    iw = iw_ref[...]

    for c in range(n_chunks):
        @pl.when(c * ck <= last)
        def _():
            sa = jnp.maximum(_dot_nt(iqs_ref[...], ika_ref[ck * c:ck * (c + 1), :]), 0.0)
            sb = jnp.maximum(_dot_nt(iqs_ref[...], ikb_ref[ck * c:ck * (c + 1), :]), 0.0)
            score = jnp.zeros((tq, ck), F32)
            for pr in range(n_pairs):
                score = score + iw[:, 2 * pr:2 * pr + 1] * sa[tq * pr:tq * (pr + 1)]
                score = score + iw[:, 2 * pr + 1:2 * pr + 2] * sb[tq * pr:tq * (pr + 1)]
            key_ref[c] = _sort_key(jnp.where((ck * c + lane) <= qpos, score, -jnp.inf))

        @pl.when(c * ck > last)
        def _():
            key_ref[c] = jnp.full((tq, ck), NEG_INF_KEY, jnp.int32)

    def write(c, lo, hi, value):
        sel_ref[c, :, lo:hi] = value

    _select_topk([key_ref[c] for c in range(n_chunks)], topk, write, 256)

    _init_softmax_state(m_ref, l_ref, acc_ref)
    for c in range(n_chunks):
        @pl.when(c * ck <= last)
        def _():
            k = kv_ref[ck * c:ck * (c + 1), 0:HEAD_DIM].astype(BF16)
            v = kv_ref[ck * c:ck * (c + 1), HEAD_DIM:2 * HEAD_DIM].astype(BF16)
            s = _dot_nt(qas_ref[...], k) * HEAD_DIM ** -0.5
            chosen = jnp.where((ck * c + lane) <= qpos, sel_ref[c], 0.0)
            state = _online_softmax(s, _tile_rows(chosen, h_a) > 0.5, v, (m_ref[...], l_ref[...], acc_ref[...]))
            m_ref[...], l_ref[...], acc_ref[...] = state

    out = acc_ref[...] / l_ref[...]
    for h in range(h_a):
        o_ref[:, HEAD_DIM * h:HEAD_DIM * (h + 1)] = out[tq * h:tq * (h + 1)].astype(o_ref.dtype)


def dsa_prompt(iq, iw, ika, ikb, qa, kv, batch, t, h_a, topk):
    tq = _pick(t, 128, 16)
    nq = t // tq
    ck = _pick(t, 512, 256)
    qrow = lambda w: pl.BlockSpec((tq, w), lambda b, i: (b * nq + i, 0))
    krow = lambda w: pl.BlockSpec((t, w), lambda b, i: (b, 0))
    return pl.pallas_call(
        functools.partial(_dsa_prompt_kernel, tq=tq, h_a=h_a, topk=topk, ck=ck),
        grid=(batch, nq),
        in_specs=[qrow(H_IDX * D_IDX), qrow(LANES), krow(LANES), krow(LANES), qrow(h_a * HEAD_DIM),
                  krow(2 * HEAD_DIM)],
        out_specs=qrow(h_a * HEAD_DIM),
        out_shape=jax.ShapeDtypeStruct((batch * t, h_a * HEAD_DIM), BF16),
        scratch_shapes=[pltpu.VMEM((H_IDX // 2 * tq, LANES), BF16), pltpu.VMEM((h_a * tq, HEAD_DIM), BF16),
                        pltpu.VMEM((t // ck, tq, ck), jnp.int32), pltpu.VMEM((t // ck, tq, ck), F32),
                        pltpu.VMEM((h_a * tq, 1), F32), pltpu.VMEM((h_a * tq, 1), F32),
                        pltpu.VMEM((h_a * tq, HEAD_DIM), F32)],
        compiler_params=_params("parallel", "parallel"),
        name="dsa_prompt",
    )(iq, iw, ika, ikb, qa, kv)


def _mla_prompt_kernel(q_ref, lat_ref, wuv_ref, o_ref, qs_ref, *, tq, h_b, ck):
    i = pl.program_id(1)
    rows = h_b * tq
    _stack_heads(q_ref, qs_ref, h_b, tq, KV_RANK + LANES)
    qpos = _tile_rows(i * tq + lax.broadcasted_iota(jnp.int32, (tq, 1), 0), h_b)
    lane = lax.broadcasted_iota(jnp.int32, (1, ck), 1)
    scale = (D_NOPE + D_ROPE_B) ** -0.5

    def body(n, state):
        m_old, l_old, acc = state
        start = pl.multiple_of(n * ck, ck)
        c = lat_ref[pl.ds(start, ck), 0:KV_RANK].astype(BF16)
        kr = lat_ref[pl.ds(start, ck), KV_RANK:KV_RANK + D_ROPE_B].astype(BF16)
        s = (_dot_nt(qs_ref[:, 0:KV_RANK], c) + _dot_nt(qs_ref[:, KV_RANK:KV_RANK + D_ROPE_B], kr)) * scale
        valid = (start + lane) <= qpos
        m_new = jnp.maximum(m_old, jnp.max(jnp.where(valid, s, NEG), axis=-1, keepdims=True))
        alpha = jnp.exp(m_old - m_new)
        p = jnp.where(valid, jnp.exp(s - m_new), 0.0)
        l_new = alpha * l_old + jnp.sum(p, axis=-1, keepdims=True)
        return m_new, l_new, alpha * acc + _dot(p.astype(BF16), c)

    n_chunks = (i * tq + tq - 1) // ck + 1
    init = (jnp.full((rows, 1), NEG, F32), jnp.zeros((rows, 1), F32), jnp.zeros((rows, KV_RANK), F32))
    _, l_fin, acc = lax.fori_loop(0, n_chunks, body, init)
    ctx = (acc / l_fin).astype(BF16)
    for h in range(h_b):
        o_ref[:, D_VB * h:D_VB * (h + 1)] = _dot(ctx[tq * h:tq * (h + 1)], wuv_ref[h]).astype(o_ref.dtype)


def mla_prompt(qlat, lat, wuv, batch, t, h_b):
    tq = _pick(t, 128, 16)
    nq = t // tq
    ck = _pick(t, 512, 256)
    return pl.pallas_call(
        functools.partial(_mla_prompt_kernel, tq=tq, h_b=h_b, ck=ck),
        grid=(batch, nq),
        in_specs=[pl.BlockSpec((tq, qlat.shape[1]), lambda b, i: (b * nq + i, 0)),
                  pl.BlockSpec((t, lat.shape[1]), lambda b, i: (b, 0)),
                  pl.BlockSpec(wuv.shape, lambda b, i: (0, 0, 0))],
        out_specs=pl.BlockSpec((tq, h_b * D_VB), lambda b, i: (b * nq + i, 0)),
        out_shape=jax.ShapeDtypeStruct((batch * t, h_b * D_VB), BF16),
        scratch_shapes=[pltpu.VMEM((h_b * tq, KV_RANK + LANES), BF16)],
        compiler_params=_params("parallel", "parallel"),
        name="mla_prompt",
    )(qlat, lat, wuv)


def _later_matrix(blk):
    ri = lax.broadcasted_iota(jnp.int32, (blk, blk), 0)
    ci = lax.broadcasted_iota(jnp.int32, (blk, blk), 1)
    return jnp.where(ri > ci, 1.0, 0.0).astype(BF16)


def _log_keep(neg_z):
    return jnp.minimum(neg_z, 0.0) - jnp.log(1.0 + jnp.exp(-jnp.abs(neg_z)))


def _stick_block(neg_z, before, later, carry):
    lk = _log_keep(neg_z)
    if before is not None:
        lk = jnp.where(before, lk, 0.0)
    hi = lk.astype(BF16)
    lo = (lk - hi.astype(F32)).astype(BF16)
    tail = _dot(hi, later) + _dot(lo, later) + carry
    a = jnp.exp(lk - neg_z + tail)
    if before is not None:
        a = jnp.where(before, a, 0.0)
    return a, carry + jnp.sum(lk, axis=-1, keepdims=True)


def _sb_prompt_kernel(q_ref, k_ref, v_ref, o_ref, qs_ref, *, tq, heads, blk):
    i = pl.program_id(2)
    rows = heads * tq
    later = _later_matrix(blk)
    _stack_heads(q_ref, qs_ref, heads, tq, HEAD_DIM)
    qpos = _tile_rows(i * tq + lax.broadcasted_iota(jnp.int32, (tq, 1), 0), heads)
    lane = lax.broadcasted_iota(jnp.int32, (1, blk), 1)
    last_block = (i * tq) // blk

    def block(start, masked, state):
        carry, acc = state
        k = k_ref[pl.ds(start, blk), :].astype(BF16)
        v = v_ref[pl.ds(start, blk), :].astype(BF16)
        neg_z = _dot_nt(qs_ref[...], k) * -(HEAD_DIM ** -0.5)
        a, carry = _stick_block(neg_z, (start + lane) < qpos if masked else None, later, carry)
        return carry, acc + _dot(a.astype(BF16), v)

    state = (jnp.zeros((rows, 1), F32), jnp.zeros((rows, HEAD_DIM), F32))
    state = block(pl.multiple_of(last_block * blk, blk), True, state)
    _, acc = lax.fori_loop(
        0, last_block, lambda n, st: block(pl.multiple_of((last_block - 1 - n) * blk, blk), False, st), state)
    for h in range(heads):
        o_ref[:, HEAD_DIM * h:HEAD_DIM * (h + 1)] = acc[tq * h:tq * (h + 1)].astype(o_ref.dtype)


def sb_prompt(proj, batch, t, h_c):
    tq = _pick(t, 128, 16)
    nq = t // tq
    heads = h_c // KVH_C
    blk = _pick(t, 256, LANES)
    assert blk % tq == 0, "a query tile must sit inside one key block"
    qcols = heads * HEAD_DIM
    kcol0 = h_c
    return pl.pallas_call(
        functools.partial(_sb_prompt_kernel, tq=tq, heads=heads, blk=blk),
        grid=(batch, KVH_C, nq),
        in_specs=[pl.BlockSpec((tq, qcols), lambda b, g, i: (b * nq + i, g)),
                  pl.BlockSpec((t, HEAD_DIM), lambda b, g, i: (b, kcol0 + g)),
                  pl.BlockSpec((t, HEAD_DIM), lambda b, g, i: (b, kcol0 + KVH_C + g))],
        out_specs=pl.BlockSpec((tq, qcols), lambda b, g, i: (b * nq + i, g)),
        out_shape=jax.ShapeDtypeStruct((batch * t, h_c * HEAD_DIM), BF16),
        scratch_shapes=[pltpu.VMEM((heads * tq, HEAD_DIM), BF16)],
        compiler_params=_params("parallel", "parallel", "parallel"),
        name="sb_prompt",
    )(proj, proj, proj)


def _page_specs(rows, cols, layer, pages_per_step, n_steps, reverse=False, n_seq=1):
    specs = []
    for g in range(n_seq):
        for r in range(pages_per_step):
            if reverse:
                imap = lambda b, j, pt, g=g, r=r: (layer, pt[b * n_seq + g, (n_steps - 1 - j) * pages_per_step + r],
                                                    0, 0)
            else:
                imap = lambda b, j, pt, g=g, r=r: (layer, pt[b * n_seq + g, j * pages_per_step + r], 0, 0)
            specs.append(pl.BlockSpec((None, None, rows, cols), imap))
    return specs


def _cat_lanes(refs, lo, hi):
    return jnp.concatenate([r[lo:hi, :] for r in refs], axis=1).astype(BF16)


def _cat_rows(refs, first, stride):
    return jnp.concatenate([r[pl.ds(first, PAGE_SIZE, stride=stride), :] for r in refs], axis=0).astype(BF16)


def _online_softmax(s, valid, v, state, v_feature_major=False):
    m_old, l_old, acc = state
    m_new = jnp.maximum(m_old, jnp.max(jnp.where(valid, s, NEG), axis=-1, keepdims=True))
    alpha = jnp.exp(m_old - m_new)
    p = jnp.where(valid, jnp.exp(s - m_new), 0.0)
    pv = _dot_nt(p.astype(BF16), v) if v_feature_major else _dot(p.astype(BF16), v)
    return m_new, alpha * l_old + jnp.sum(p, axis=-1, keepdims=True), alpha * acc + pv


def _init_softmax_state(m_ref, l_ref, acc_ref):
    m_ref[...] = jnp.full_like(m_ref, NEG)
    l_ref[...] = jnp.zeros_like(l_ref)
    acc_ref[...] = jnp.zeros_like(acc_ref)


def _idx_mla_decode_kernel(pt_ref, iq_ref, iw_ref, inew_ref, q_ref, lnew_ref, *rest, n_pg, n_seq):
    idx_pages = rest[:n_seq * n_pg]
    lat_pages = rest[n_seq * n_pg:2 * n_seq * n_pg]
    score_ref, snew_ref, ctx_ref, m_ref, l_ref, acc_ref = rest[2 * n_seq * n_pg:]
    j = pl.program_id(1)

    @pl.when(j == 0)
    def _():
        _init_softmax_state(m_ref, l_ref, acc_ref)

    scale = (D_NOPE + D_ROPE_B) ** -0.5
    for g in range(n_seq):
        iq = iq_ref[g]
        iw = iw_ref[g]
        keys_t = _cat_lanes(idx_pages[g * n_pg:(g + 1) * n_pg], 0, D_IDX)
        score_ref[g] = jnp.sum(iw * jnp.maximum(_dot(iq, keys_t), 0.0), axis=0, keepdims=True)
        snew_ref[g] = jnp.sum(iw * jnp.maximum(_dot(iq, inew_ref[g].astype(BF16)), 0.0), axis=0, keepdims=True)

        qn = q_ref[g, :, 0:KV_RANK]
        qr = q_ref[g, :, KV_RANK:KV_RANK + D_ROPE_B]
        pages = lat_pages[g * n_pg:(g + 1) * n_pg]
        c_t = _cat_lanes(pages, 0, KV_RANK)
        kr_t = _cat_lanes(pages, KV_RANK, KV_RANK + D_ROPE_B)
        s = (_dot(qn, c_t) + _dot(qr, kr_t)) * scale
        state = _online_softmax(s, jnp.full(s.shape, True), c_t, (m_ref[g], l_ref[g], acc_ref[g]), True)
        m_ref[g], l_ref[g], acc_ref[g] = state

    @pl.when(j == pl.num_programs(1) - 1)
    def _():
        for g in range(n_seq):
            qn = q_ref[g, :, 0:KV_RANK]
            qr = q_ref[g, :, KV_RANK:KV_RANK + D_ROPE_B]
            cn_t = lnew_ref[g, 0:KV_RANK, :].astype(BF16)
            krn_t = lnew_ref[g, KV_RANK:KV_RANK + D_ROPE_B, :].astype(BF16)
            sn = (_dot(qn, cn_t) + _dot(qr, krn_t)) * scale
            first = lax.broadcasted_iota(jnp.int32, sn.shape, 1) < 1
            _, l_fin, acc = _online_softmax(sn, first, cn_t, (m_ref[g], l_ref[g], acc_ref[g]), True)
            ctx_ref[g] = (acc / l_fin).astype(ctx_ref.dtype)


def idx_mla_decode(page_table, iq, iw, new_idx, pool_idx, q, new_lat, pool_lat, layer):
    b, n_pages = page_table.shape
    h = q.shape[1]
    width = KV_RANK + D_ROPE_B
    n_seq = _pick(b, 2, 1)
    n_pg = _pick(n_pages, 16, 1)
    n_steps = n_pages // n_pg
    per_seq = lambda *tail: pl.BlockSpec((n_seq,) + tail, lambda b_, j, pt: (b_,) + (0,) * len(tail))
    grid_spec = pltpu.PrefetchScalarGridSpec(
        num_scalar_prefetch=1,
        grid=(b // n_seq, n_steps),
        in_specs=[per_seq(H_IDX, D_IDX), per_seq(H_IDX, 1), per_seq(D_IDX, PAGE_SIZE),
                  per_seq(h, q.shape[2]), per_seq(width, PAGE_SIZE)]
        + _page_specs(D_IDX, PAGE_SIZE, layer, n_pg, n_steps, n_seq=n_seq)
        + _page_specs(width, PAGE_SIZE, layer, n_pg, n_steps, n_seq=n_seq),
        out_specs=[pl.BlockSpec((n_seq, 1, n_pg * PAGE_SIZE), lambda b_, j, pt: (b_, 0, j)),
                   per_seq(1, PAGE_SIZE), per_seq(h, KV_RANK)],
        scratch_shapes=[pltpu.VMEM((n_seq, h, 1), F32), pltpu.VMEM((n_seq, h, 1), F32),
                        pltpu.VMEM((n_seq, h, KV_RANK), F32)],
    )
    return pl.pallas_call(
        functools.partial(_idx_mla_decode_kernel, n_pg=n_pg, n_seq=n_seq),
        grid_spec=grid_spec,
        out_shape=[jax.ShapeDtypeStruct((b, 1, n_pages * PAGE_SIZE), F32),
                   jax.ShapeDtypeStruct((b, 1, PAGE_SIZE), F32),
                   jax.ShapeDtypeStruct((b, h, KV_RANK), BF16)],
        compiler_params=_params("parallel", "arbitrary"),
        name="idx_mla_decode",
    )(page_table, iq, iw, new_idx, q, new_lat, *([pool_idx] * (n_seq * n_pg)), *([pool_lat] * (n_seq * n_pg)))


def _select_decode_kernel(score_ref, sel_ref, *, n_valid, topk):
    score = score_ref[...]
    kpos = lax.broadcasted_iota(jnp.int32, (1, score.shape[1]), 1)
    key = _sort_key(jnp.where(kpos < n_valid, score, -jnp.inf))

    def write(c, lo, hi, value):
        sel_ref[:, lo:hi] = value

    _select_topk([key], topk, write, LANES)


def select_decode(score, n_valid, topk):
    b, n = score.shape
    tb = _pick(b, 32, 8)
    return pl.pallas_call(
        functools.partial(_select_decode_kernel, n_valid=n_valid, topk=topk),
        grid=(b // tb,),
        in_specs=[pl.BlockSpec((tb, n), lambda i: (i, 0))],
        out_specs=pl.BlockSpec((tb, n), lambda i: (i, 0)),
        out_shape=jax.ShapeDtypeStruct((b, n), F32),
        compiler_params=_params("parallel"),
        name="select_decode",
    )(score)


def _dsa_decode_kernel(pt_ref, q_ref, sel_ref, selnew_ref, new_ref, *rest, n_pg, n_seq):
    pages = rest[:n_seq * n_pg]
    o_ref, m_ref, l_ref, acc_ref = rest[n_seq * n_pg:]
    j = pl.program_id(1)

    @pl.when(j == 0)
    def _():
        _init_softmax_state(m_ref, l_ref, acc_ref)

    scale = HEAD_DIM ** -0.5
    for g in range(n_seq):
        k = _cat_rows(pages[g * n_pg:(g + 1) * n_pg], 0, 2)
        v = _cat_rows(pages[g * n_pg:(g + 1) * n_pg], 1, 2)
        s = _dot_nt(q_ref[g], k) * scale
        m_ref[g], l_ref[g], acc_ref[g] = _online_softmax(s, sel_ref[g] > 0.5, v, (m_ref[g], l_ref[g], acc_ref[g]))

    @pl.when(j == pl.num_programs(1) - 1)
    def _():
        for g in range(n_seq):
            kn = new_ref[g, pl.ds(0, PAGE_SIZE, stride=2), :].astype(BF16)
            vn = new_ref[g, pl.ds(1, PAGE_SIZE, stride=2), :].astype(BF16)
            sn = _dot_nt(q_ref[g], kn) * scale
            _, l_fin, acc = _online_softmax(sn, selnew_ref[g] > 0.5, vn, (m_ref[g], l_ref[g], acc_ref[g]))
            o_ref[g] = (acc / l_fin).astype(o_ref.dtype)


def dsa_decode(page_table, q, sel, new_page, pool, layer):
    b, n_pages = page_table.shape
    h = q.shape[1]
    n_seq = _pick(b, 2, 1)
    n_pg = _pick(n_pages, 16, 1)
    n_steps = n_pages // n_pg
    grid_spec = pltpu.PrefetchScalarGridSpec(
        num_scalar_prefetch=1,
        grid=(b // n_seq, n_steps),
        in_specs=[pl.BlockSpec((n_seq, h, HEAD_DIM), lambda b_, j, pt: (b_, 0, 0)),
                  pl.BlockSpec((n_seq, 1, n_pg * PAGE_SIZE), lambda b_, j, pt: (b_, 0, j)),
                  pl.BlockSpec((n_seq, 1, PAGE_SIZE), lambda b_, j, pt: (b_, 0, n_pages)),
                  pl.BlockSpec((n_seq, 2 * PAGE_SIZE, HEAD_DIM), lambda b_, j, pt: (b_, 0, 0))]
        + _page_specs(2 * PAGE_SIZE, HEAD_DIM, layer, n_pg, n_steps, n_seq=n_seq),
        out_specs=pl.BlockSpec((n_seq, h, HEAD_DIM), lambda b_, j, pt: (b_, 0, 0)),
        scratch_shapes=[pltpu.VMEM((n_seq, h, 1), F32), pltpu.VMEM((n_seq, h, 1), F32),
                        pltpu.VMEM((n_seq, h, HEAD_DIM), F32)],
    )
    return pl.pallas_call(
        functools.partial(_dsa_decode_kernel, n_pg=n_pg, n_seq=n_seq),
        grid_spec=grid_spec,
        out_shape=jax.ShapeDtypeStruct((b, h, HEAD_DIM), BF16),
        compiler_params=_params("parallel", "arbitrary"),
        name="dsa_decode",
    )(page_table, q, sel, sel, new_page, *([pool] * (n_seq * n_pg)))


def _sb_decode_kernel(pt_ref, q_ref, *rest, n_pg, blk):
    pages = rest[:n_pg]
    o_ref, carry_ref, acc_ref = rest[n_pg:]
    j = pl.program_id(1)

    @pl.when(j == 0)
    def _():
        carry_ref[...] = jnp.zeros_like(carry_ref)
        acc_ref[...] = jnp.zeros_like(acc_ref)

    later = _later_matrix(blk)
    heads = q_ref.shape[1]
    n_blocks = n_pg * PAGE_SIZE // blk
    z_blocks, values = [], []
    for g in range(KVH_C):
        z = _dot_nt(q_ref[g], _cat_rows(pages, g, 2 * KVH_C)) * -(HEAD_DIM ** -0.5)
        z_blocks += [z[:, blk * n:blk * (n + 1)] for n in range(n_blocks)]
        values.append(_cat_rows(pages, KVH_C + g, 2 * KVH_C))
    neg_z = jnp.concatenate(z_blocks, axis=0)
    rows = neg_z.shape[0]
    lk = _log_keep(neg_z)
    hi = lk.astype(BF16)
    lo = (lk - hi.astype(F32)).astype(BF16)
    within = _dot(jnp.concatenate([hi, lo], axis=0), later)
    total = jnp.sum(lk, axis=-1, keepdims=True)
    carries = [None] * (KVH_C * n_blocks)
    for g in range(KVH_C):
        running = carry_ref[g]
        for n in reversed(range(n_blocks)):
            b = g * n_blocks + n
            carries[b] = running
            running = running + total[heads * b:heads * (b + 1)]
        carry_ref[g] = running
    tail = within[:rows] + within[rows:] + jnp.concatenate(carries, axis=0)
    a = jnp.exp(lk - neg_z + tail).astype(BF16)
    for g in range(KVH_C):
        a_g = jnp.concatenate([a[heads * (g * n_blocks + n):heads * (g * n_blocks + n + 1)]
                               for n in range(n_blocks)], axis=1)
        acc_ref[g] = acc_ref[g] + _dot(a_g, values[g])

    @pl.when(j == pl.num_programs(1) - 1)
    def _():
        o_ref[...] = acc_ref[...].astype(o_ref.dtype)


def sb_decode(page_table, q, pool, layer):
    b, n_pages = page_table.shape
    heads = q.shape[2]
    n_pg = _pick(n_pages, 16, 1)
    n_steps = n_pages // n_pg
    blk = _pick(n_pg * PAGE_SIZE, 256, LANES)
    grid_spec = pltpu.PrefetchScalarGridSpec(
        num_scalar_prefetch=1,
        grid=(b, n_steps),
        in_specs=[pl.BlockSpec((None, KVH_C, heads, HEAD_DIM), lambda b_, j, pt: (b_, 0, 0, 0))]
        + _page_specs(2 * KVH_C * PAGE_SIZE, HEAD_DIM, layer, n_pg, n_steps, True),
        out_specs=pl.BlockSpec((None, KVH_C, heads, HEAD_DIM), lambda b_, j, pt: (b_, 0, 0, 0)),
        scratch_shapes=[pltpu.VMEM((KVH_C, heads, 1), F32), pltpu.VMEM((KVH_C, heads, HEAD_DIM), F32)],
    )
    return pl.pallas_call(
        functools.partial(_sb_decode_kernel, n_pg=n_pg, blk=blk),
        grid_spec=grid_spec,
        out_shape=jax.ShapeDtypeStruct((b, KVH_C, heads, HEAD_DIM), BF16),
        compiler_params=_params("parallel", "arbitrary"),
        name="sb_decode",
    )(page_table, q, *([pool] * n_pg))


def _pad_cols(w, width):
    return jnp.pad(w, ((0, 0), (0, width - w.shape[1])))


def _pad_row(g, width):
    return jnp.pad(g, (0, width - g.shape[0])).reshape(1, width)


def _new_page_rows(rows, rows_per_pos):
    b = rows.shape[0]
    r = rows.reshape(b, rows_per_pos, -1)
    return jnp.pad(r, ((0, 0), (0, rows_per_pos * (PAGE_SIZE - 1)), (0, 0)))


def _new_page_feature_major(rows):
    return jnp.pad(rows[:, :, None], ((0, 0), (0, 0), (0, PAGE_SIZE - 1)))


def even_layer(x, tabs, n_prompt, batch, t, page_table, layer, cache_a_kv, cache_a_idx, cache_b_latent,
               e_norm, e_w_in, a_qnorm, a_knorm, idx_knorm, b_qa_norm, b_wq_b, b_kv_norm, b_wkv_b,
               b_qnorm, b_krnorm, e_w_out):
    m, d = x.shape
    h_a = d // 2 // HEAD_DIM
    h_b = h_a
    past_len = page_table.shape[1] * PAGE_SIZE
    dec_b = m - n_prompt

    widths = (h_a * HEAD_DIM, HEAD_DIM, HEAD_DIM, H_IDX * D_IDX, H_IDX, D_IDX, Q_RANK, KV_RANK, D_ROPE_B)
    src = np.cumsum((0,) + (h_a * HEAD_DIM, HEAD_DIM, HEAD_DIM, H_IDX * D_IDX, H_IDX, D_IDX, Q_RANK, KV_RANK))
    padded = tuple(-(-w // LANES) * LANES for w in widths)
    offs = tuple(int(o) for o in np.cumsum((0,) + padded)[:-1])
    total = -(-sum(padded) // 512) * 512
    cols = [_pad_cols(e_w_in[:, int(s):int(s) + w], p) for s, w, p in zip(src, widths, padded)]
    w_in = _pad_cols(jnp.concatenate(cols, axis=1), total).astype(BF16)

    h = rms_norm_rows(x, e_norm)
    proj = matmul(h, w_in)
    gains = (a_qnorm.reshape(1, -1), a_knorm.reshape(1, -1), _pad_row(idx_knorm, LANES),
             b_qa_norm.reshape(1, -1), b_kv_norm.reshape(1, -1), _pad_row(b_krnorm, LANES))
    qa, kv, iq, iw, ik, ika, ikb, qlo, lat = even_post(proj, tabs, gains, h_a, offs)

    wq = b_wq_b.reshape(Q_RANK, h_b, D_NOPE + D_ROPE_B)
    wq_nope = wq[:, :, :D_NOPE].reshape(Q_RANK, h_b * D_NOPE)
    wq_rope = jnp.pad(wq[:, :, D_NOPE:], ((0, 0), (0, 0), (0, LANES - D_ROPE_B))).reshape(Q_RANK, h_b * LANES)
    qb = matmul(qlo, jnp.concatenate([wq_nope, wq_rope], axis=1).astype(BF16))
    wuk = jnp.transpose(b_wkv_b[:, :, :D_NOPE], (1, 2, 0)).astype(BF16)
    wuv = jnp.transpose(b_wkv_b[:, :, D_NOPE:], (1, 0, 2)).astype(BF16)
    qlat = mla_q(qb, tabs, b_qnorm[:D_NOPE].reshape(1, -1), _pad_row(b_qnorm[D_NOPE:], LANES), wuk, h_b)

    topk_p = min(TOPK_MAX, t // 4)
    oa_p = dsa_prompt(iq, iw, ika, ikb, qa, kv, batch, t, h_a, topk_p)
    ob_p = mla_prompt(qlat, lat, wuv, batch, t, h_b)

    topk_s = min(TOPK_MAX, (past_len + 1) // 4)
    pool_kv = cache_a_kv.reshape(cache_a_kv.shape[:2] + (2 * PAGE_SIZE, HEAD_DIM))
    pool_idx = jnp.swapaxes(cache_a_idx, 2, 3)
    pool_lat = jnp.swapaxes(cache_b_latent, 2, 3)
    iq_s = iq[n_prompt:].reshape(dec_b, H_IDX, D_IDX)
    iw_s = iw[n_prompt:, :H_IDX].reshape(dec_b, H_IDX, 1)
    sc_old, sc_new, ctx_s = idx_mla_decode(
        page_table, iq_s, iw_s, _new_page_feature_major(ik[n_prompt:]), pool_idx,
        qlat[n_prompt:].reshape(dec_b, h_b, KV_RANK + LANES), _new_page_feature_major(lat[n_prompt:]), pool_lat, layer)
    sel = select_decode(jnp.concatenate([sc_old[:, 0], sc_new[:, 0]], axis=1), past_len + 1, topk_s)
    oa_s = dsa_decode(page_table, qa[n_prompt:].reshape(dec_b, h_a, HEAD_DIM), sel[:, None, :],
                      _new_page_rows(kv[n_prompt:], 2), pool_kv, layer)
    ob_s = head_out(ctx_s.reshape(dec_b, h_b * KV_RANK), wuv)

    mix_in = jnp.concatenate([jnp.concatenate([oa_p, ob_p], axis=1),
                              jnp.concatenate([oa_s.reshape(dec_b, -1), ob_s], axis=1)], axis=0)
    x = matmul_residual(mix_in, e_w_out.astype(BF16), x, 1.0, tm_target=832)
    return x, kv, ik, lat


def odd_layer(x, n_prompt, batch, t, page_table, layer, cache_c_kv, o_norm, o_w_in, o_w_out):
    m, d = x.shape
    h_c = d // HEAD_DIM
    dec_b = m - n_prompt
    h = rms_norm_rows(x, o_norm)
    proj = matmul(h, o_w_in.astype(BF16))
    o_p = sb_prompt(proj, batch, t, h_c)
    q_s = proj[n_prompt:, :h_c * HEAD_DIM].astype(BF16).reshape(dec_b, KVH_C, h_c // KVH_C, HEAD_DIM)
    pool = cache_c_kv.reshape(cache_c_kv.shape[:2] + (2 * KVH_C * PAGE_SIZE, HEAD_DIM))
    o_s = sb_decode(page_table, q_s, pool, layer)
    mix_in = jnp.concatenate([o_p, o_s.reshape(dec_b, -1)], axis=0)
    x = matmul_residual(mix_in, o_w_out.astype(BF16), x, 1.0, tm_target=832)
    return x, proj[:, h_c * HEAD_DIM:]


def kernel(x_prompt, x_sample, cache_a_kv, cache_a_idx, cache_b_latent, cache_c_kv, page_table, ffn1_norm, ffn1_w_in, ffn1_w_out, ffn2_norm, ffn2_w_in, ffn2_w_out, e_norm, e_w_in, e_a_qnorm, e_a_knorm, e_idx_knorm, e_b_qa_norm, e_b_wq_b, e_b_kv_norm, e_b_wkv_b, e_b_qnorm, e_b_krnorm, e_w_out, o_norm, o_w_in, o_w_out):
    batch, t, d = x_prompt.shape
    dec_b, dec_t, _ = x_sample.shape
    assert dec_t == 1, "one new token per sampled sequence"
    depth = ffn1_norm.shape[0]
    n_prompt = batch * t
    past_len = page_table.shape[1] * PAGE_SIZE
    x = jnp.concatenate([x_prompt.reshape(n_prompt, d), x_sample.reshape(dec_b, d)], axis=0)

    pos = jnp.concatenate([jnp.tile(jnp.arange(t, dtype=F32), batch),
                           jnp.full((dec_b,), past_len, F32)]).reshape(-1, 1)
    rows = jnp.stack([_rope_rows(HEAD_DIM // 4, HEAD_DIM), _rope_rows(D_IDX // 4, D_IDX),
                      _rope_rows(D_ROPE_B, LANES)])
    tabs = rope_tables(pos, rows)

    w1_in, w1_out = ffn1_w_in, ffn1_w_out.astype(BF16)
    w2_in, w2_out = ffn2_w_in, ffn2_w_out.astype(BF16)
    akv, aidx, blat, ckv = [], [], [], []
    for layer in range(depth):
        x = ffn_half_step(x, ffn1_norm[layer], w1_in, w1_out, layer)
        if layer % 2 == 0:
            e = layer // 2
            x, kv, ik, lat = even_layer(
                x, tabs, n_prompt, batch, t, page_table, e, cache_a_kv, cache_a_idx, cache_b_latent,
                e_norm[e], e_w_in[e], e_a_qnorm[e], e_a_knorm[e], e_idx_knorm[e], e_b_qa_norm[e], e_b_wq_b[e],
                e_b_kv_norm[e], e_b_wkv_b[e], e_b_qnorm[e], e_b_krnorm[e], e_w_out[e])
            akv.append(kv)
            aidx.append(ik)
            blat.append(lat)
        else:
            o = layer // 2
            x, kvc = odd_layer(x, n_prompt, batch, t, page_table, o, cache_c_kv, o_norm[o], o_w_in[o], o_w_out[o])
            ckv.append(kvc)
        x = ffn_half_step(x, ffn2_norm[layer], w2_in, w2_out, layer,
                          split_at=n_prompt if layer == depth - 1 else None)
    y_prompt, y_sample = x

    def split(rows_list, tail):
        a = jnp.stack(rows_list)
        return (a[:, :n_prompt].reshape((len(rows_list), batch, t) + tail),
                a[:, n_prompt:].reshape((len(rows_list), dec_b, dec_t) + tail))

    akv_p, akv_s = split(akv, (2, 1, HEAD_DIM))
    aidx_p, aidx_s = split(aidx, (D_IDX,))
    blat_p, blat_s = split(blat, (KV_RANK + D_ROPE_B,))
    ckv_p, ckv_s = split(ckv, (2, KVH_C, HEAD_DIM))
    return (y_prompt.reshape(batch, t, d), y_sample.reshape(dec_b, dec_t, d),
            akv_p, akv_s, aidx_p, aidx_s, blat_p, blat_s, ckv_p, ckv_s)
```
